```python
import jax, jax.numpy as jnp
from jax import lax
import numpy as np

D_MODEL = 1024
BATCH = 32
SEQ = 2048
DEPTH = 1

D_MIX = D_MODEL
D_RET = D_MIX // 2
D_GLA = D_MIX - D_RET
RET_HEADS = 4
RET_DK = D_RET // RET_HEADS
RET_DV = D_RET // RET_HEADS
RET_CHUNK = 128
GLA_HEADS = 4
GLA_DK_TOTAL = D_GLA // 2
GLA_DK = GLA_DK_TOTAL // GLA_HEADS
GLA_DV = D_GLA // GLA_HEADS
GLA_CHUNK = 64
GLA_GATE_RANK = 16
GLA_GATE_TAU = 16.0
ROPE_THETA = 10000.0
EPS = 1e-6
D_IN = 4 * D_RET + 2 * GLA_DK_TOTAL + 2 * D_GLA + GLA_GATE_RANK

kernel_name = "hybrid_retention_gla_adaln_layer"


def _rms(x):
    xf = x.astype(jnp.float32)
    return (xf * lax.rsqrt(jnp.mean(xf * xf, axis=-1, keepdims=True) + EPS)).astype(x.dtype)


def _rotary(t):
    S, d = t.shape[1], t.shape[-1]
    inv_freq = ROPE_THETA ** (-jnp.arange(0, d, 2, dtype=jnp.float32) / d)
    ang = jnp.arange(S, dtype=jnp.float32)[:, None] * inv_freq[None, :]
    cos = jnp.cos(ang)[None, :, None, :]
    sin = jnp.sin(ang)[None, :, None, :]
    t1, t2 = jnp.split(t.astype(jnp.float32), 2, axis=-1)
    return jnp.concatenate([t1 * cos - t2 * sin, t1 * sin + t2 * cos], axis=-1).astype(t.dtype)


def _retention(q, k, v):
    B, S, H, dk = q.shape
    dv = v.shape[-1]
    C = RET_CHUNK
    nc = S // C
    log_g = jnp.log(1.0 - 2.0 ** (-5.0 - jnp.arange(H, dtype=jnp.float32)))
    q = q.reshape(B, nc, C, H, dk)
    k = k.reshape(B, nc, C, H, dk)
    v = v.reshape(B, nc, C, H, dv)
    idx = jnp.arange(C, dtype=jnp.float32)
    diff = idx[:, None] - idx[None, :]
    decay = jnp.where(diff[None] >= 0,
                      jnp.exp(jnp.maximum(diff, 0.0)[None] * log_g[:, None, None]), 0.0)
    scores = jnp.einsum('bnihd,bnjhd->bnhij', q, k) * decay
    inner = jnp.einsum('bnhij,bnjhe->bnihe', scores, v)
    zeta = jnp.exp((C - 1.0 - idx)[None, :] * log_g[:, None])
    kv = jnp.einsum('bnjhd,hj,bnjhe->nbhde', k, zeta, v)
    chunk_decay = jnp.exp(C * log_g)[None, :, None, None]

    def step(R, kv_n):
        return R * chunk_decay + kv_n, R

    _, R_prev = lax.scan(step, jnp.zeros((B, H, dk, dv), kv.dtype), kv)
    xi = jnp.exp((idx + 1.0)[None, :] * log_g[:, None])
    cross = jnp.einsum('bnihd,nbhde,hi->bnihe', q, R_prev, xi)
    return (inner + cross).reshape(B, S, H, dv)


def _gla(q, k, v, log_a):
    B, S, H, dk = q.shape
    dv = v.shape[-1]
    C = GLA_CHUNK
    nc = S // C
    q = q.reshape(B, nc, C, H, dk).astype(jnp.float32)
    k = k.reshape(B, nc, C, H, dk).astype(jnp.float32)
    v = v.reshape(B, nc, C, H, dv)
    b = jnp.cumsum(log_a.reshape(B, nc, C, H, dk).astype(jnp.float32), axis=2)
    q_e = q * jnp.exp(b)
    k_e = k * jnp.exp(-b)
    mask = jnp.tril(jnp.ones((C, C), dtype=bool))
    scores = jnp.where(mask, jnp.einsum('bnihd,bnjhd->bnhij', q_e, k_e), 0.0)
    inner = jnp.einsum('bnhij,bnjhe->bnihe', scores, v)
    b_last = b[:, :, -1:]
    kv = jnp.einsum('bnjhd,bnjhe->nbhde', k * jnp.exp(b_last - b), v)
    a_chunk = jnp.transpose(jnp.exp(b_last[:, :, 0]), (1, 0, 2, 3))

    def step(St, inp):
        kv_n, a_n = inp
        return St * a_n[..., None] + kv_n, St

    _, S_prev = lax.scan(step, jnp.zeros((B, H, dk, dv), kv.dtype), (kv, a_chunk))
    cross = jnp.einsum('bnihd,nbhde->bnihe', q_e, S_prev)
    return (inner + cross).reshape(B, S, H, dv)


def setup_inputs(seed: int = 0) -> dict:
    key = jax.random.key(seed)
    ks = jax.random.split(key, 12)
    f32 = jnp.float32
    x = jax.random.normal(ks[0], (BATCH, SEQ, D_MODEL), f32)
    c = jax.random.normal(ks[1], (BATCH, D_MODEL), f32)
    norm_gain = 1.0 + 0.02 * jax.random.normal(ks[2], (DEPTH, D_MODEL), f32)
    w_ada = jax.random.normal(ks[3], (DEPTH, D_MODEL, 3 * D_MODEL), f32) * D_MODEL ** -0.5
    b_ada = 0.02 * jax.random.normal(ks[4], (DEPTH, 3 * D_MODEL), f32)
    w_in = jax.random.normal(ks[5], (DEPTH, D_MODEL, D_IN), f32) * D_MODEL ** -0.5
    w_gate_up = jax.random.normal(ks[6], (DEPTH, GLA_GATE_RANK, GLA_DK_TOTAL), f32) * GLA_GATE_RANK ** -0.5
    b_gate_up = 0.02 * jax.random.normal(ks[7], (DEPTH, GLA_DK_TOTAL), f32)
    gla_norm_gain = 1.0 + 0.02 * jax.random.normal(ks[8], (DEPTH, GLA_DV), f32)
    w_out = jax.random.normal(ks[9], (DEPTH, D_MIX, D_MODEL), f32) * D_MIX ** -0.5
    final_gain = 1.0 + 0.02 * jax.random.normal(ks[10], (D_MODEL,), f32)
    return {"x": x, "c": c, "norm_gain": norm_gain, "w_ada": w_ada, "b_ada": b_ada,
            "w_in": w_in, "w_gate_up": w_gate_up, "b_gate_up": b_gate_up,
            "gla_norm_gain": gla_norm_gain, "w_out": w_out, "final_gain": final_gain}


def reference(x, c, norm_gain, w_ada, b_ada, w_in, w_gate_up, b_gate_up, gla_norm_gain, w_out, final_gain):
    B, S, D = x.shape
    cs = jax.nn.silu(c)
    for l in range(DEPTH):
        ada = cs @ w_ada[l] + b_ada[l]
        shift, scale, gate = jnp.split(ada, 3, axis=-1)
        h = _rms(x) * norm_gain[l] * (1.0 + scale[:, None, :]) + shift[:, None, :]
        p = h @ w_in[l]
        o = np.cumsum([0, D_RET, D_RET, D_RET, D_RET, GLA_DK_TOTAL, GLA_DK_TOTAL, D_GLA, D_GLA, GLA_GATE_RANK])
        rq, rk, rv, rz, gq, gk, gv, gz, glr = [p[..., o[i]:o[i + 1]] for i in range(9)]
        rq = _rotary(rq.reshape(B, S, RET_HEADS, RET_DK))
        rk = _rotary(rk.reshape(B, S, RET_HEADS, RET_DK)) * (RET_DK ** -0.5)
        y_ret = _retention(rq, rk, rv.reshape(B, S, RET_HEADS, RET_DV))
        y_ret = _rms(y_ret) * jax.nn.silu(rz.reshape(B, S, RET_HEADS, RET_DV).astype(jnp.float32))
        log_a = jax.nn.log_sigmoid((glr @ w_gate_up[l] + b_gate_up[l]).astype(jnp.float32)) / GLA_GATE_TAU
        y_gla = _gla(gq.reshape(B, S, GLA_HEADS, GLA_DK) * (GLA_DK ** -0.5),
                     gk.reshape(B, S, GLA_HEADS, GLA_DK),
                     gv.reshape(B, S, GLA_HEADS, GLA_DV),
                     log_a.reshape(B, S, GLA_HEADS, GLA_DK))
        y_gla = _rms(y_gla) * gla_norm_gain[l] * jax.nn.silu(gz.reshape(B, S, GLA_HEADS, GLA_DV).astype(jnp.float32))
        mix = jnp.concatenate([y_ret.reshape(B, S, D_RET), y_gla.reshape(B, S, D_GLA)], axis=-1).astype(x.dtype)
        x = x + gate[:, None, :] * (mix @ w_out[l])
    return _rms(x) * final_gain
```

```python
import functools

import numpy as np
import jax
import jax.numpy as jnp
from jax import lax
from jax.experimental import pallas as pl
from jax.experimental.pallas import tpu as pltpu

D_MODEL = 1024
D_RET = 512
D_GLA = 512
RET_HEADS = 4
RET_DK = 128
RET_DV = 128
RET_CHUNK = 128
GLA_HEADS = 4
GLA_DK_TOTAL = 256
GLA_DK = 64
GLA_DV = 128
GLA_CHUNK = 64
GLA_GATE_RANK = 16
GLA_GATE_TAU = 16.0
ROPE_THETA = 10000.0
EPS = 1e-6
D_IN = 4 * D_RET + 2 * GLA_DK_TOTAL + 2 * D_GLA + GLA_GATE_RANK

LANES = 128
D_IN_PAD = -(-D_IN // LANES) * LANES
GLR_PAD = D_IN_PAD - (D_IN - GLA_GATE_RANK)

O_RQ, O_RK, O_RV, O_RZ = 0, 512, 1024, 1536
O_GQ, O_GK, O_GV, O_GZ, O_GLR = 2048, 2304, 2560, 3072, 3584

SEQ_TILE = 256
VMEM_LIMIT_BYTES = 48 * 1024 * 1024

F32 = jnp.float32
BF16 = jnp.bfloat16


def _split3(a):
    hi = a.astype(BF16)
    r1 = a - hi.astype(F32)
    mid = r1.astype(BF16)
    lo = (r1 - mid.astype(F32)).astype(BF16)
    return hi, mid, lo


def _dot(a, b):
    return jnp.dot(a, b, preferred_element_type=F32)


def _dot_nt(a, b):
    return lax.dot_general(a, b, (((1,), (1,)), ((), ())), preferred_element_type=F32)


def _ada_kernel(c_ref, w_ref, b_ref, o_ref):
    c = c_ref[...]
    cs = c * jax.nn.sigmoid(c)
    w = w_ref[...]
    c_hi = cs.astype(BF16)
    c_lo = (cs - c_hi.astype(F32)).astype(BF16)
    w_hi = w.astype(BF16)
    w_lo = (w - w_hi.astype(F32)).astype(BF16)
    o_ref[...] = _dot(c_hi, w_hi) + _dot(c_lo, w_hi) + _dot(c_hi, w_lo) + b_ref[...]


def _rms_lanes(y):
    return y * lax.rsqrt(jnp.mean(y * y, axis=-1, keepdims=True) + EPS)


def _layer_kernel(x_ref, ada_ref, ng_ref, fg_ref, gg_ref, bgu_ref, wgu_ref, win_ref, wout_ref,
                  cos_ref, sin_ref, decay_ref, xi_ref, zeta_ref,
                  o_ref,
                  q_s, qx_s, k_s, kz_s, v_s, rz_s,
                  gq_s, gk_s, gv_s, gz_s, la_s, mix_s, r_state, g_state,
                  *, ret_chunk_decay):
    T = x_ref.shape[1]

    @pl.when(pl.program_id(1) == 0)
    def _():
        r_state[...] = jnp.zeros_like(r_state)
        g_state[...] = jnp.zeros_like(g_state)

    x = x_ref[0]
    shift = ada_ref[0, 0:1, :]
    scale = ada_ref[0, 1:2, :]
    gate = ada_ref[0, 2:3, :]
    h = _rms_lanes(x) * (ng_ref[...] * (1.0 + scale)) + shift
    hb = h.astype(BF16)

    def proj(lo, width):
        return _dot(hb, win_ref[:, lo:lo + width])

    cos = cos_ref[...]
    sin = sin_ref[...]
    rq = proj(O_RQ, D_RET)
    rk = proj(O_RK, D_RET)
    for hh in range(RET_HEADS):
        sl = slice(hh * RET_DK, (hh + 1) * RET_DK)
        qh = rq[:, sl]
        qr = qh * cos + pltpu.roll(qh, RET_DK // 2, axis=1) * sin
        kh = rk[:, sl]
        kr = kh * cos + pltpu.roll(kh, RET_DK // 2, axis=1) * sin
        q_s[:, sl] = qr.astype(BF16)
        qx_s[:, sl] = (qr * xi_ref[:, sl]).astype(BF16)
        k_s[:, sl] = kr.astype(BF16)
        kz_s[:, sl] = (kr * zeta_ref[:, sl]).astype(BF16)
    v_s[...] = proj(O_RV, D_RET).astype(BF16)
    rz_s[...] = proj(O_RZ, D_RET)

    def ret_chunk(ci, carry):
        rows = pl.ds(pl.multiple_of(ci * RET_CHUNK, RET_CHUNK), RET_CHUNK)
        for hh in range(RET_HEADS):
            sl = slice(hh * RET_DK, (hh + 1) * RET_DK)
            q = q_s[rows, sl]
            k = k_s[rows, sl]
            v = v_s[rows, sl]
            state = r_state[hh]
            sc = _dot_nt(q, k) * decay_ref[hh]
            lhs = jnp.concatenate([sc.astype(BF16), qx_s[rows, sl]], axis=1)
            rhs = jnp.concatenate([v, state.astype(BF16)], axis=0)
            y = _dot(lhs, rhs)
            kv = lax.dot_general(kz_s[rows, sl], v, (((0,), (0,)), ((), ())),
                                 preferred_element_type=F32)
            r_state[hh] = state * ret_chunk_decay[hh] + kv
            z = rz_s[rows, sl]
            mix_s[rows, sl] = (_rms_lanes(y) * (z * jax.nn.sigmoid(z))).astype(BF16)
        return carry

    lax.fori_loop(0, T // RET_CHUNK, ret_chunk, 0)

    gq_s[...] = proj(O_GQ, GLA_DK_TOTAL) * (GLA_DK ** -0.5)
    gk_s[...] = proj(O_GK, GLA_DK_TOTAL)
    gv_s[...] = proj(O_GV, D_GLA).astype(BF16)
    gz_s[...] = proj(O_GZ, D_GLA)
    glr = proj(O_GLR, GLR_PAD)
    zg = _dot(glr.astype(BF16), wgu_ref[...]) + bgu_ref[...]
    la_s[...] = (jnp.minimum(zg, 0.0) - jnp.log1p(jnp.exp(-jnp.abs(zg)))) * (1.0 / GLA_GATE_TAU)

    C = GLA_CHUNK
    r_i = lax.broadcasted_iota(jnp.int32, (C, C), 0)
    c_i = lax.broadcasted_iota(jnp.int32, (C, C), 1)
    tri = jnp.where(r_i >= c_i, 1.0, 0.0).astype(BF16)
    r_st = lax.broadcasted_iota(jnp.int32, (GLA_HEADS * C, C), 0) & (C - 1)
    c_st = lax.broadcasted_iota(jnp.int32, (GLA_HEADS * C, C), 1)
    causal = r_st >= c_st
    lane_head = lax.broadcasted_iota(jnp.int32, (C, GLA_DK_TOTAL), 1) // GLA_DK
    gg = gg_ref[...]

    def gla_chunk(ci, carry):
        rows = pl.ds(pl.multiple_of(ci * C, C), C)
        la = la_s[rows, :]
        hi, mid, lo = _split3(la)
        b = _dot(tri, hi) + _dot(tri, mid) + _dot(tri, lo)
        b_last = b[C - 1:C, :]
        gk = gk_s[rows, :]
        q_e = gq_s[rows, :] * jnp.exp(b)
        k_e = (gk * jnp.exp(-b)).astype(BF16)
        k_w = gk * jnp.exp(b_last - b)
        q_stack = jnp.concatenate(
            [jnp.where(lane_head == hh, q_e, 0.0) for hh in range(GLA_HEADS)], axis=0).astype(BF16)
        sc = jnp.where(causal, _dot_nt(q_stack, k_e), 0.0).astype(BF16)
        state = g_state[...]
        cross = _dot(q_stack, state.astype(BF16))
        v = gv_s[rows, :]
        kwt = k_w.T.astype(BF16)
        kvs = []
        for hh in range(GLA_HEADS):
            sl = slice(hh * GLA_DV, (hh + 1) * GLA_DV)
            rh = slice(hh * C, (hh + 1) * C)
            y = _dot(sc[rh], v[:, sl]) + cross[rh]
            z = gz_s[rows, sl]
            mix_s[rows, D_RET + hh * GLA_DV:D_RET + (hh + 1) * GLA_DV] = (
                _rms_lanes(y) * gg[:, sl] * (z * jax.nn.sigmoid(z))).astype(BF16)
            kvs.append(_dot(kwt[hh * GLA_DK:(hh + 1) * GLA_DK], v[:, sl]))
        a_col = jnp.broadcast_to(jnp.exp(b_last), (GLA_DV, GLA_DK_TOTAL)).T
        g_state[...] = state * a_col + jnp.concatenate(kvs, axis=0)
        return carry

    lax.fori_loop(0, T // C, gla_chunk, 0)

    o = _dot(mix_s[...], wout_ref[...])
    xn = x + gate * o
    o_ref[0] = _rms_lanes(xn) * fg_ref[...]


def _retention_tables(T):
    H, C = RET_HEADS, RET_CHUNK
    log_g = np.log(1.0 - 2.0 ** (-5.0 - np.arange(H, dtype=np.float64)))
    idx = np.arange(C, dtype=np.float64)
    diff = idx[:, None] - idx[None, :]
    k_scale = RET_DK ** -0.5
    decay = np.where(diff[None] >= 0, np.exp(np.maximum(diff, 0.0)[None] * log_g[:, None, None]), 0.0)
    zeta = np.exp((C - 1.0 - idx)[None, :] * log_g[:, None])
    xi = np.exp((idx + 1.0)[None, :] * log_g[:, None])
    chunk_decay = tuple(float(v) for v in np.exp(C * log_g))

    def full(t):
        return np.tile(np.repeat(t.T, RET_DV, axis=1), (T // C, 1))

    return ((decay * k_scale).astype(np.float32), full(xi).astype(np.float32),
            (full(zeta) * k_scale).astype(np.float32), chunk_decay)


def kernel(x, c, norm_gain, w_ada, b_ada, w_in, w_gate_up, b_gate_up, gla_norm_gain, w_out, final_gain):
    B, S, D = x.shape
    T = SEQ_TILE
    assert D == D_MODEL and S % T == 0 and T % RET_CHUNK == 0

    ada = pl.pallas_call(
        _ada_kernel,
        grid=(3,),
        in_specs=[pl.BlockSpec((B, D), lambda j: (0, 0)),
                  pl.BlockSpec((D, D), lambda j: (0, j)),
                  pl.BlockSpec((1, D), lambda j: (0, j))],
        out_specs=pl.BlockSpec((B, D), lambda j: (0, j)),
        out_shape=jax.ShapeDtypeStruct((B, 3 * D), F32),
        name="adaln_proj",
    )(c, w_ada[0], b_ada[0][None, :])
    ada3 = ada.reshape(B, 3, D)

    w_in_p = jnp.pad(w_in[0], ((0, 0), (0, D_IN_PAD - D_IN))).astype(BF16)
    w_out_b = w_out[0].astype(BF16)
    wgu_p = jnp.pad(w_gate_up[0], ((0, GLR_PAD - GLA_GATE_RANK), (0, 0))).astype(BF16)
    bgu = b_gate_up[0][None, :]
    gg = jnp.tile(gla_norm_gain[0], GLA_HEADS)[None, :]

    inv_freq = ROPE_THETA ** (-jnp.arange(0, RET_DK, 2, dtype=F32) / RET_DK)
    ang = jnp.arange(S, dtype=F32)[:, None] * inv_freq[None, :]
    cos2 = jnp.concatenate([jnp.cos(ang), jnp.cos(ang)], axis=-1)
    sin2 = jnp.concatenate([-jnp.sin(ang), jnp.sin(ang)], axis=-1)

    decay, xi_full, zeta_full, chunk_decay = _retention_tables(T)

    const2 = lambda b, s: (0, 0)
    const3 = lambda b, s: (0, 0, 0)
    body = functools.partial(_layer_kernel, ret_chunk_decay=chunk_decay)
    return pl.pallas_call(
        body,
        grid=(B, S // T),
        in_specs=[
            pl.BlockSpec((1, T, D), lambda b, s: (b, s, 0)),
            pl.BlockSpec((1, 3, D), lambda b, s: (b, 0, 0)),
            pl.BlockSpec((1, D), const2),
            pl.BlockSpec((1, D), const2),
            pl.BlockSpec((1, D_GLA), const2),
            pl.BlockSpec((1, GLA_DK_TOTAL), const2),
            pl.BlockSpec((GLR_PAD, GLA_DK_TOTAL), const2),
            pl.BlockSpec((D, D_IN_PAD), const2),
            pl.BlockSpec((D, D), const2),
            pl.BlockSpec((T, RET_DK), lambda b, s: (s, 0)),
            pl.BlockSpec((T, RET_DK), lambda b, s: (s, 0)),
            pl.BlockSpec((RET_HEADS, RET_CHUNK, RET_CHUNK), const3),
            pl.BlockSpec((T, D_RET), const2),
            pl.BlockSpec((T, D_RET), const2),
        ],
        out_specs=pl.BlockSpec((1, T, D), lambda b, s: (b, s, 0)),
        out_shape=jax.ShapeDtypeStruct((B, S, D), F32),
        scratch_shapes=[
            pltpu.VMEM((T, D_RET), BF16),
            pltpu.VMEM((T, D_RET), BF16),
            pltpu.VMEM((T, D_RET), BF16),
            pltpu.VMEM((T, D_RET), BF16),
            pltpu.VMEM((T, D_RET), BF16),
            pltpu.VMEM((T, D_RET), F32),
            pltpu.VMEM((T, GLA_DK_TOTAL), F32),
            pltpu.VMEM((T, GLA_DK_TOTAL), F32),
            pltpu.VMEM((T, D_GLA), BF16),
            pltpu.VMEM((T, D_GLA), F32),
            pltpu.VMEM((T, GLA_DK_TOTAL), F32),
            pltpu.VMEM((T, D_MODEL), BF16),
            pltpu.VMEM((RET_HEADS, RET_DK, RET_DV), F32),
            pltpu.VMEM((GLA_HEADS * GLA_DK, GLA_DV), F32),
        ],
        compiler_params=pltpu.CompilerParams(
            dimension_semantics=("arbitrary", "arbitrary"),
            vmem_limit_bytes=VMEM_LIMIT_BYTES),
        name="retention_gla_layer",
    )(x, ada3, norm_gain, final_gain[None, :], gg, bgu, wgu_p, w_in_p, w_out_b,
      cos2, sin2, jnp.asarray(decay), jnp.asarray(xi_full), jnp.asarray(zeta_full))
```

```python
import functools

import numpy as np
import jax
import jax.numpy as jnp
from jax import lax
from jax.experimental import pallas as pl
from jax.experimental.pallas import tpu as pltpu

D_MODEL = 1024
D_RET = 512
D_GLA = 512
RET_HEADS = 4
RET_DK = 128
RET_DV = 128
RET_CHUNK = 128
GLA_HEADS = 4
GLA_DK_TOTAL = 256
GLA_DK = 64
GLA_DV = 128
GLA_CHUNK = 64
GLA_GATE_RANK = 16
GLA_GATE_TAU = 16.0
ROPE_THETA = 10000.0
EPS = 1e-6
D_IN = 4 * D_RET + 2 * GLA_DK_TOTAL + 2 * D_GLA + GLA_GATE_RANK

LANES = 128
D_IN_PAD = -(-D_IN // LANES) * LANES
GLR_PAD = D_IN_PAD - (D_IN - GLA_GATE_RANK)

O_RQ, O_RK, O_RV, O_RZ = 0, 512, 1024, 1536
O_GQ, O_GK, O_GV, O_GZ, O_GLR = 2048, 2304, 2560, 3072, 3584

SEQ_TILE = 256
VMEM_LIMIT_BYTES = 48 * 1024 * 1024

F32 = jnp.float32
BF16 = jnp.bfloat16


def _split3(a):
    hi = a.astype(BF16)
    r1 = a - hi.astype(F32)
    mid = r1.astype(BF16)
    lo = (r1 - mid.astype(F32)).astype(BF16)
    return hi, mid, lo


def _dot(a, b):
    return jnp.dot(a, b, preferred_element_type=F32)


def _dot_nt(a, b):
    return lax.dot_general(a, b, (((1,), (1,)), ((), ())), preferred_element_type=F32)


def _ada_kernel(c_ref, w_ref, b_ref, o_ref):
    c = c_ref[...]
    cs = c * jax.nn.sigmoid(c)
    w = w_ref[...]
    c_hi = cs.astype(BF16)
    c_lo = (cs - c_hi.astype(F32)).astype(BF16)
    w_hi = w.astype(BF16)
    w_lo = (w - w_hi.astype(F32)).astype(BF16)
    o_ref[...] = _dot(c_hi, w_hi) + _dot(c_lo, w_hi) + _dot(c_hi, w_lo) + b_ref[...]


def _rms_lanes(y):
    return y * lax.rsqrt(jnp.mean(y * y, axis=-1, keepdims=True) + EPS)


def _layer_kernel(x_ref, ada_ref, ng_ref, fg_ref, gg_ref, bgu_ref, wgu_ref, win_ref, wout_ref,
                  cos_ref, sin_ref, decay_ref, xi_ref, zeta_ref,
                  o_ref,
                  q_s, qx_s, k_s, kz_s, v_s, rz_s,
                  gq_s, gk_s, gv_s, gz_s, la_s, mix_s, r_state, g_state,
                  *, ret_chunk_decay):
    T = x_ref.shape[1]

    @pl.when(pl.program_id(1) == 0)
    def _():
        r_state[...] = jnp.zeros_like(r_state)
        g_state[...] = jnp.zeros_like(g_state)

    x = x_ref[0]
    shift = ada_ref[0, 0:1, :]
    scale = ada_ref[0, 1:2, :]
    gate = ada_ref[0, 2:3, :]
    h = _rms_lanes(x) * (ng_ref[...] * (1.0 + scale)) + shift
    hb = h.astype(BF16)

    def proj(lo, width):
        return _dot(hb, win_ref[:, lo:lo + width])

    cos = cos_ref[...]
    sin = sin_ref[...]
    rq = proj(O_RQ, D_RET)
    rk = proj(O_RK, D_RET)
    for hh in range(RET_HEADS):
        sl = slice(hh * RET_DK, (hh + 1) * RET_DK)
        qh = rq[:, sl]
        qr = qh * cos + pltpu.roll(qh, RET_DK // 2, axis=1) * sin
        kh = rk[:, sl]
        kr = kh * cos + pltpu.roll(kh, RET_DK // 2, axis=1) * sin
        q_s[:, sl] = qr.astype(BF16)
        qx_s[:, sl] = (qr * xi_ref[:, sl]).astype(BF16)
        k_s[:, sl] = kr.astype(BF16)
        kz_s[:, sl] = (kr * zeta_ref[:, sl]).astype(BF16)
    v_s[...] = proj(O_RV, D_RET).astype(BF16)
    rz_s[...] = proj(O_RZ, D_RET)

    def ret_chunk(ci, carry):
        rows = pl.ds(pl.multiple_of(ci * RET_CHUNK, RET_CHUNK), RET_CHUNK)
        for hh in range(RET_HEADS):
            sl = slice(hh * RET_DK, (hh + 1) * RET_DK)
            q = q_s[rows, sl]
            k = k_s[rows, sl]
            v = v_s[rows, sl]
            state = r_state[hh]
            sc = _dot_nt(q, k) * decay_ref[hh]
            lhs = jnp.concatenate([sc.astype(BF16), qx_s[rows, sl]], axis=1)
            rhs = jnp.concatenate([v, state.astype(BF16)], axis=0)
            y = _dot(lhs, rhs)
            kv = lax.dot_general(kz_s[rows, sl], v, (((0,), (0,)), ((), ())),
                                 preferred_element_type=F32)
            r_state[hh] = state * ret_chunk_decay[hh] + kv
            z = rz_s[rows, sl]
            mix_s[rows, sl] = (_rms_lanes(y) * (z * jax.nn.sigmoid(z))).astype(BF16)
        return carry

    lax.fori_loop(0, T // RET_CHUNK, ret_chunk, 0, unroll=True)

    gq_s[...] = proj(O_GQ, GLA_DK_TOTAL) * (GLA_DK ** -0.5)
    gk_s[...] = proj(O_GK, GLA_DK_TOTAL)
    gv_s[...] = proj(O_GV, D_GLA).astype(BF16)
    gz_s[...] = proj(O_GZ, D_GLA)
    glr = proj(O_GLR, GLR_PAD)
    zg = _dot(glr.astype(BF16), wgu_ref[...]) + bgu_ref[...]
    la_s[...] = (jnp.minimum(zg, 0.0) - jnp.log1p(jnp.exp(-jnp.abs(zg)))) * (1.0 / GLA_GATE_TAU)

    C = GLA_CHUNK
    r_i = lax.broadcasted_iota(jnp.int32, (C, C), 0)
    c_i = lax.broadcasted_iota(jnp.int32, (C, C), 1)
    tri = jnp.where(r_i >= c_i, 1.0, 0.0).astype(BF16)
    r_st = lax.broadcasted_iota(jnp.int32, (GLA_HEADS * C, C), 0) & (C - 1)
    c_st = lax.broadcasted_iota(jnp.int32, (GLA_HEADS * C, C), 1)
    causal = r_st >= c_st
    lane_head = lax.broadcasted_iota(jnp.int32, (C, GLA_DK_TOTAL), 1) // GLA_DK
    gg = gg_ref[...]

    def gla_chunk(ci, carry):
        rows = pl.ds(pl.multiple_of(ci * C, C), C)
        la = la_s[rows, :]
        hi, mid, lo = _split3(la)
        b = _dot(tri, hi) + _dot(tri, mid) + _dot(tri, lo)
        b_last = b[C - 1:C, :]
        gk = gk_s[rows, :]
        q_e = gq_s[rows, :] * jnp.exp(b)
        k_e = (gk * jnp.exp(-b)).astype(BF16)
        k_w = gk * jnp.exp(b_last - b)
        q_stack = jnp.concatenate(
            [jnp.where(lane_head == hh, q_e, 0.0) for hh in range(GLA_HEADS)], axis=0).astype(BF16)
        sc = jnp.where(causal, _dot_nt(q_stack, k_e), 0.0).astype(BF16)
        state = g_state[...]
        cross = _dot(q_stack, state.astype(BF16))
        v = gv_s[rows, :]
        kwt = k_w.T.astype(BF16)
        kvs = []
        for hh in range(GLA_HEADS):
            sl = slice(hh * GLA_DV, (hh + 1) * GLA_DV)
            rh = slice(hh * C, (hh + 1) * C)
            y = _dot(sc[rh], v[:, sl]) + cross[rh]
            z = gz_s[rows, sl]
            mix_s[rows, D_RET + hh * GLA_DV:D_RET + (hh + 1) * GLA_DV] = (
                _rms_lanes(y) * gg[:, sl] * (z * jax.nn.sigmoid(z))).astype(BF16)
            kvs.append(_dot(kwt[hh * GLA_DK:(hh + 1) * GLA_DK], v[:, sl]))
        a_col = jnp.broadcast_to(jnp.exp(b_last), (GLA_DV, GLA_DK_TOTAL)).T
        g_state[...] = state * a_col + jnp.concatenate(kvs, axis=0)
        return carry

    lax.fori_loop(0, T // C, gla_chunk, 0, unroll=True)

    o = _dot(mix_s[...], wout_ref[...])
    xn = x + gate * o
    o_ref[0] = _rms_lanes(xn) * fg_ref[...]


def _retention_tables(T):
    H, C = RET_HEADS, RET_CHUNK
    log_g = np.log(1.0 - 2.0 ** (-5.0 - np.arange(H, dtype=np.float64)))
    idx = np.arange(C, dtype=np.float64)
    diff = idx[:, None] - idx[None, :]
    k_scale = RET_DK ** -0.5
    decay = np.where(diff[None] >= 0, np.exp(np.maximum(diff, 0.0)[None] * log_g[:, None, None]), 0.0)
    zeta = np.exp((C - 1.0 - idx)[None, :] * log_g[:, None])
    xi = np.exp((idx + 1.0)[None, :] * log_g[:, None])
    chunk_decay = tuple(float(v) for v in np.exp(C * log_g))

    def full(t):
        return np.tile(np.repeat(t.T, RET_DV, axis=1), (T // C, 1))

    return ((decay * k_scale).astype(np.float32), full(xi).astype(np.float32),
            (full(zeta) * k_scale).astype(np.float32), chunk_decay)


def kernel(x, c, norm_gain, w_ada, b_ada, w_in, w_gate_up, b_gate_up, gla_norm_gain, w_out, final_gain):
    B, S, D = x.shape
    T = SEQ_TILE
    assert D == D_MODEL and S % T == 0 and T % RET_CHUNK == 0

    ada = pl.pallas_call(
        _ada_kernel,
        grid=(3,),
        in_specs=[pl.BlockSpec((B, D), lambda j: (0, 0)),
                  pl.BlockSpec((D, D), lambda j: (0, j)),
                  pl.BlockSpec((1, D), lambda j: (0, j))],
        out_specs=pl.BlockSpec((B, D), lambda j: (0, j)),
        out_shape=jax.ShapeDtypeStruct((B, 3 * D), F32),
        name="adaln_proj",
    )(c, w_ada[0], b_ada[0][None, :])
    ada3 = ada.reshape(B, 3, D)

    w_in_p = jnp.pad(w_in[0], ((0, 0), (0, D_IN_PAD - D_IN))).astype(BF16)
    w_out_b = w_out[0].astype(BF16)
    wgu_p = jnp.pad(w_gate_up[0], ((0, GLR_PAD - GLA_GATE_RANK), (0, 0))).astype(BF16)
    bgu = b_gate_up[0][None, :]
    gg = jnp.tile(gla_norm_gain[0], GLA_HEADS)[None, :]

    inv_freq = ROPE_THETA ** (-jnp.arange(0, RET_DK, 2, dtype=F32) / RET_DK)
    ang = jnp.arange(S, dtype=F32)[:, None] * inv_freq[None, :]
    cos2 = jnp.concatenate([jnp.cos(ang), jnp.cos(ang)], axis=-1)
    sin2 = jnp.concatenate([-jnp.sin(ang), jnp.sin(ang)], axis=-1)

    decay, xi_full, zeta_full, chunk_decay = _retention_tables(T)

    const2 = lambda b, s: (0, 0)
    const3 = lambda b, s: (0, 0, 0)
    body = functools.partial(_layer_kernel, ret_chunk_decay=chunk_decay)
    return pl.pallas_call(
        body,
        grid=(B, S // T),
        in_specs=[
            pl.BlockSpec((1, T, D), lambda b, s: (b, s, 0)),
            pl.BlockSpec((1, 3, D), lambda b, s: (b, 0, 0)),
            pl.BlockSpec((1, D), const2),
            pl.BlockSpec((1, D), const2),
            pl.BlockSpec((1, D_GLA), const2),
            pl.BlockSpec((1, GLA_DK_TOTAL), const2),
            pl.BlockSpec((GLR_PAD, GLA_DK_TOTAL), const2),
            pl.BlockSpec((D, D_IN_PAD), const2),
            pl.BlockSpec((D, D), const2),
            pl.BlockSpec((T, RET_DK), lambda b, s: (s, 0)),
            pl.BlockSpec((T, RET_DK), lambda b, s: (s, 0)),
            pl.BlockSpec((RET_HEADS, RET_CHUNK, RET_CHUNK), const3),
            pl.BlockSpec((T, D_RET), const2),
            pl.BlockSpec((T, D_RET), const2),
        ],
        out_specs=pl.BlockSpec((1, T, D), lambda b, s: (b, s, 0)),
        out_shape=jax.ShapeDtypeStruct((B, S, D), F32),
        scratch_shapes=[
            pltpu.VMEM((T, D_RET), BF16),
            pltpu.VMEM((T, D_RET), BF16),
            pltpu.VMEM((T, D_RET), BF16),
            pltpu.VMEM((T, D_RET), BF16),
            pltpu.VMEM((T, D_RET), BF16),
            pltpu.VMEM((T, D_RET), F32),
            pltpu.VMEM((T, GLA_DK_TOTAL), F32),
            pltpu.VMEM((T, GLA_DK_TOTAL), F32),
            pltpu.VMEM((T, D_GLA), BF16),
            pltpu.VMEM((T, D_GLA), F32),
            pltpu.VMEM((T, GLA_DK_TOTAL), F32),
            pltpu.VMEM((T, D_MODEL), BF16),
            pltpu.VMEM((RET_HEADS, RET_DK, RET_DV), F32),
            pltpu.VMEM((GLA_HEADS * GLA_DK, GLA_DV), F32),
        ],
        compiler_params=pltpu.CompilerParams(
            dimension_semantics=("arbitrary", "arbitrary"),
            vmem_limit_bytes=VMEM_LIMIT_BYTES),
        name="retention_gla_layer",
    )(x, ada3, norm_gain, final_gain[None, :], gg, bgu, wgu_p, w_in_p, w_out_b,
      cos2, sin2, jnp.asarray(decay), jnp.asarray(xi_full), jnp.asarray(zeta_full))
```

```python
import functools
from typing import NamedTuple

import numpy as np
import jax
import jax.numpy as jnp
from jax import lax
from jax.experimental import pallas as pl
from jax.experimental.pallas import tpu as pltpu

D_MODEL = 1024
D_RET = 512
D_GLA = 512
RET_HEADS = 4
RET_DK = 128
RET_DV = 128
RET_CHUNK = 128
GLA_HEADS = 4
GLA_DK_TOTAL = 256
GLA_DK = 64
GLA_DV = 128
GLA_CHUNK = 64
GLA_GATE_RANK = 16
GLA_GATE_TAU = 16.0
ROPE_THETA = 10000.0
EPS = 1e-6
D_IN = 4 * D_RET + 2 * GLA_DK_TOTAL + 2 * D_GLA + GLA_GATE_RANK

LANES = 128
MXU_COLS = 256
D_IN_PAD = -(-D_IN // LANES) * LANES
GLR_PAD = D_IN_PAD - (D_IN - GLA_GATE_RANK)

O_RQ, O_RK, O_RV, O_RZ = 0, 512, 1024, 1536
O_GQ, O_GK, O_GV, O_GZ, O_GLR = 2048, 2304, 2560, 3072, 3584

SEQ_TILE = 256
VMEM_LIMIT_BYTES = 48 * 1024 * 1024

F32 = jnp.float32
BF16 = jnp.bfloat16


class _Slot(NamedTuple):
    q: object
    qx: object
    k: object
    kz: object
    v: object
    rg: object
    qe: object
    ke: object
    kw: object
    gv: object
    gg: object
    ea: object


def _slot_scratch(T):
    shapes = dict(q=((T, D_RET), BF16), qx=((T, D_RET), BF16), k=((T, D_RET), BF16), kz=((T, D_RET), BF16),
                  v=((T, D_RET), BF16), rg=((T, D_RET), F32),
                  qe=((T, GLA_DK_TOTAL), BF16), ke=((T, GLA_DK_TOTAL), BF16), kw=((T, GLA_DK_TOTAL), BF16),
                  gv=((T, D_GLA), BF16), gg=((T, D_GLA), F32), ea=((8, GLA_DK_TOTAL), F32))
    return [pltpu.VMEM(*shapes[f]) for f in _Slot._fields]


def _split3(a):
    hi = a.astype(BF16)
    r1 = a - hi.astype(F32)
    mid = r1.astype(BF16)
    lo = (r1 - mid.astype(F32)).astype(BF16)
    return hi, mid, lo


def _dot(a, b):
    return jnp.dot(a, b, preferred_element_type=F32)


def _dot_nt(a, b):
    return lax.dot_general(a, b, (((1,), (1,)), ((), ())), preferred_element_type=F32)


def _dot_tn(a, b):
    return lax.dot_general(a, b, (((0,), (0,)), ((), ())), preferred_element_type=F32)


def _silu(z):
    return z * jax.nn.sigmoid(z)


def _ada_kernel(c_ref, w_ref, b_ref, o_ref):
    cs = _silu(c_ref[...])
    w = w_ref[...]
    c_hi = cs.astype(BF16)
    c_lo = (cs - c_hi.astype(F32)).astype(BF16)
    w_hi = w.astype(BF16)
    w_lo = (w - w_hi.astype(F32)).astype(BF16)
    o_ref[...] = _dot(c_hi, w_hi) + _dot(c_lo, w_hi) + _dot(c_hi, w_lo) + b_ref[...]


def _rms_lanes(y):
    return y * lax.rsqrt(jnp.mean(y * y, axis=-1, keepdims=True) + EPS)


def _interleave(first, second):
    n, m = len(first), len(second)
    order = sorted([((j + 0.5) / n, 0, j) for j in range(n)] + [((j + 0.5) / m, 1, j) for j in range(m)])
    for _, which, j in order:
        (first if which == 0 else second)[j]()


def _layer_kernel(xa_ref, xb_ref, ada_ref, ng_ref, fg_ref, gg_ref, bgu_ref, wgu_ref, win_ref, wout_ref,
                  cos_ref, sin_ref, decay_ref, xi_ref, zeta_ref, tri_ref, hmask_ref,
                  o_ref,
                  hb_s, mix_s, ssq_s, r_state, g_state, *slot_refs,
                  ret_chunk_decay, n_tiles, tiles_per_seq):
    i = pl.program_id(0)
    tile_b = jnp.maximum(i - 1, 0)
    slots = (_Slot(*slot_refs[:len(_Slot._fields)]), _Slot(*slot_refs[len(_Slot._fields):]))

    @pl.when(i == 0)
    def _():
        for ref in slot_refs:
            ref[...] = jnp.zeros_like(ref)

    @pl.when(tile_b % tiles_per_seq == 0)
    def _():
        r_state[...] = jnp.zeros_like(r_state)
        g_state[...] = jnp.zeros_like(g_state)

    for parity in range(2):
        @pl.when(i % 2 == parity)
        def _():
            _pipeline_step(xa_ref, xb_ref, ada_ref, ng_ref, fg_ref, gg_ref, bgu_ref, wgu_ref, win_ref,
                           wout_ref, cos_ref, sin_ref, decay_ref, xi_ref, zeta_ref, tri_ref, hmask_ref,
                           o_ref, hb_s, mix_s, ssq_s, r_state, g_state,
                           fill=slots[parity], drain=slots[1 - parity],
                           ret_chunk_decay=ret_chunk_decay, n_tiles=n_tiles, tiles_per_seq=tiles_per_seq)


def _pipeline_step(xa_ref, xb_ref, ada_ref, ng_ref, fg_ref, gg_ref, bgu_ref, wgu_ref, win_ref, wout_ref,
                   cos_ref, sin_ref, decay_ref, xi_ref, zeta_ref, tri_ref, hmask_ref,
                   o_ref, hb_s, mix_s, ssq_s, r_state, g_state,
                   *, fill, drain, ret_chunk_decay, n_tiles, tiles_per_seq):
    T = xa_ref.shape[1]
    i = pl.program_id(0)
    batch_a = jnp.minimum(i, n_tiles - 1) // tiles_per_seq
    batch_b = jnp.maximum(i - 1, 0) // tiles_per_seq

    def a_prologue():
        x = xa_ref[0]
        shift = ada_ref[batch_a, 0:1, :]
        scale = ada_ref[batch_a, 1:2, :]
        h = _rms_lanes(x) * (ng_ref[...] * (1.0 + scale)) + shift
        hb_s[...] = h.astype(BF16)

    def proj(lo, width):
        return _dot(hb_s[...], win_ref[:, lo:lo + width])

    def a_rotary(col, out_s, outw_s, w_ref, pair):
        def piece():
            cos = cos_ref[...]
            sin = sin_ref[...]
            p = proj(col + pair * MXU_COLS, MXU_COLS)
            for j in range(2):
                hh = 2 * pair + j
                sl = slice(hh * RET_DK, (hh + 1) * RET_DK)
                t = p[:, j * RET_DK:(j + 1) * RET_DK]
                r = t * cos + pltpu.roll(t, RET_DK // 2, axis=1) * sin
                out_s[:, sl] = r.astype(BF16)
                outw_s[:, sl] = (r * w_ref[:, sl]).astype(BF16)
        return piece

    def a_plain(col, out_s, half, act=None):
        def piece():
            sl = slice(half * MXU_COLS, (half + 1) * MXU_COLS)
            p = proj(col + half * MXU_COLS, MXU_COLS)
            if act is not None:
                p = act(p, sl)
            out_s[:, sl] = p.astype(out_s.dtype)
        return piece

    def a_gla_prep():
        gq = proj(O_GQ, GLA_DK_TOTAL) * (GLA_DK ** -0.5)
        gk = proj(O_GK, GLA_DK_TOTAL)
        glr = proj(O_GLR, GLR_PAD)
        zg = _dot(glr.astype(BF16), wgu_ref[...]) + bgu_ref[...]
        la = (jnp.minimum(zg, 0.0) - jnp.log1p(jnp.exp(-jnp.abs(zg)))) * (1.0 / GLA_GATE_TAU)
        hi, mid, lo = _split3(la)
        tri = tri_ref[...]
        b = _dot(tri, hi) + _dot(tri, mid) + _dot(tri, lo)
        C = GLA_CHUNK
        for c in range(T // C):
            rows = slice(c * C, (c + 1) * C)
            bc = b[rows]
            bl = bc[C - 1:C, :]
            fill.qe[rows, :] = (gq[rows] * jnp.exp(bc)).astype(BF16)
            fill.ke[rows, :] = (gk[rows] * jnp.exp(-bc)).astype(BF16)
            fill.kw[rows, :] = (gk[rows] * jnp.exp(bl - bc)).astype(BF16)
            fill.ea[c:c + 1, :] = jnp.exp(bl)

    def act_silu(p, sl):
        return _silu(p)

    def act_silu_gain(p, sl):
        return _silu(p) * gg_ref[:, sl]

    stage_a = [
        a_prologue,
        a_rotary(O_RQ, fill.q, fill.qx, xi_ref, 0), a_rotary(O_RQ, fill.q, fill.qx, xi_ref, 1),
        a_rotary(O_RK, fill.k, fill.kz, zeta_ref, 0), a_rotary(O_RK, fill.k, fill.kz, zeta_ref, 1),
        a_plain(O_RV, fill.v, 0), a_plain(O_RV, fill.v, 1),
        a_plain(O_RZ, fill.rg, 0, act_silu), a_plain(O_RZ, fill.rg, 1, act_silu),
        a_gla_prep,
        a_plain(O_GV, fill.gv, 0), a_plain(O_GV, fill.gv, 1),
        a_plain(O_GZ, fill.gg, 0, act_silu_gain), a_plain(O_GZ, fill.gg, 1, act_silu_gain),
    ]

    def b_ret_block(ci, hh):
        def piece():
            rows = pl.ds(ci * RET_CHUNK, RET_CHUNK)
            sl = slice(hh * RET_DK, (hh + 1) * RET_DK)
            q = drain.q[rows, sl]
            k = drain.k[rows, sl]
            v = drain.v[rows, sl]
            state = r_state[hh]
            sc = _dot_nt(q, k) * decay_ref[hh]
            lhs = jnp.concatenate([sc.astype(BF16), drain.qx[rows, sl]], axis=1)
            rhs = jnp.concatenate([v, state.astype(BF16)], axis=0)
            y = _dot(lhs, rhs)
            kv = _dot_tn(drain.kz[rows, sl], v)
            r_state[hh] = state * ret_chunk_decay[hh] + kv
            mix_s[rows, sl] = (_rms_lanes(y) * drain.rg[rows, sl]).astype(BF16)
        return piece

    C = GLA_CHUNK
    HC = GLA_HEADS * C

    def b_gla_chunk(c):
        def piece():
            rows = pl.ds(c * C, C)
            head_lanes = hmask_ref[...] > 0
            zero = jnp.zeros((HC, GLA_DK_TOTAL), BF16)

            def stack(ref):
                t = ref[rows, :]
                return jnp.where(head_lanes, jnp.concatenate([t] * GLA_HEADS, axis=0), zero)

            q_st = stack(drain.qe)
            k_st = stack(drain.ke)
            w_st = stack(drain.kw)
            v_st = jnp.concatenate(
                [drain.gv[rows, hh * GLA_DV:(hh + 1) * GLA_DV] for hh in range(GLA_HEADS)], axis=0)
            r_i = lax.broadcasted_iota(jnp.int32, (HC, HC), 0) & (C - 1)
            c_i = lax.broadcasted_iota(jnp.int32, (HC, HC), 1) & (C - 1)
            sc = jnp.where(r_i >= c_i, _dot_nt(q_st, k_st), 0.0).astype(BF16)
            state_t = g_state[...]
            y = _dot(sc, v_st) + _dot_nt(q_st, state_t.astype(BF16))
            g_state[...] = state_t * drain.ea[c:c + 1, :] + _dot_tn(v_st, w_st)
            for hh in range(GLA_HEADS):
                sl = slice(hh * GLA_DV, (hh + 1) * GLA_DV)
                mix_s[rows, D_RET + hh * GLA_DV:D_RET + (hh + 1) * GLA_DV] = (
                    _rms_lanes(y[hh * C:(hh + 1) * C]) * drain.gg[rows, sl]).astype(BF16)
        return piece

    def b_out_proj(n):
        def piece():
            sl = slice(n * MXU_COLS, (n + 1) * MXU_COLS)
            o = _dot(mix_s[...], wout_ref[:, sl])
            xn = xb_ref[0, :, sl] + ada_ref[batch_b, 2:3, sl] * o
            o_ref[0, :, sl] = xn
            part = jnp.sum(xn * xn, axis=-1, keepdims=True)
            if n == 0:
                ssq_s[...] = part
            else:
                ssq_s[...] += part
        return piece

    def b_final_norm():
        scale = lax.rsqrt(ssq_s[...] * (1.0 / D_MODEL) + EPS)
        o_ref[0] = o_ref[0] * scale * fg_ref[...]

    stage_b = (
        [b_ret_block(ci, hh) for ci in range(T // RET_CHUNK) for hh in range(RET_HEADS)]
        + [b_gla_chunk(c) for c in range(T // C)]
        + [b_out_proj(n) for n in range(D_MODEL // MXU_COLS)]
        + [b_final_norm])

    _interleave(stage_b, stage_a)


def _retention_tables(T):
    H, C = RET_HEADS, RET_CHUNK
    log_g = np.log(1.0 - 2.0 ** (-5.0 - np.arange(H, dtype=np.float64)))
    idx = np.arange(C, dtype=np.float64)
    diff = idx[:, None] - idx[None, :]
    k_scale = RET_DK ** -0.5
    decay = np.where(diff[None] >= 0, np.exp(np.maximum(diff, 0.0)[None] * log_g[:, None, None]), 0.0)
    zeta = np.exp((C - 1.0 - idx)[None, :] * log_g[:, None])
    xi = np.exp((idx + 1.0)[None, :] * log_g[:, None])
    chunk_decay = tuple(float(v) for v in np.exp(C * log_g))

    def full(t):
        return np.tile(np.repeat(t.T, RET_DV, axis=1), (T // C, 1))

    return ((decay * k_scale).astype(np.float32), full(xi).astype(np.float32),
            (full(zeta) * k_scale).astype(np.float32), chunk_decay)


def _gla_tables(T):
    C = GLA_CHUNK
    r = np.arange(T)
    tri = ((r[:, None] // C == r[None, :] // C) & (r[:, None] >= r[None, :])).astype(np.float32)
    row_head = np.arange(GLA_HEADS * C) // C
    lane_head = np.arange(GLA_DK_TOTAL) // GLA_DK
    hmask = (row_head[:, None] == lane_head[None, :]).astype(np.float32)
    return tri, hmask


def kernel(x, c, norm_gain, w_ada, b_ada, w_in, w_gate_up, b_gate_up, gla_norm_gain, w_out, final_gain):
    B, S, D = x.shape
    T = SEQ_TILE
    assert D == D_MODEL and S % T == 0 and T % RET_CHUNK == 0
    tiles_per_seq = S // T
    n_tiles = B * tiles_per_seq

    ada = pl.pallas_call(
        _ada_kernel,
        grid=(3,),
        in_specs=[pl.BlockSpec((B, D), lambda j: (0, 0)),
                  pl.BlockSpec((D, D), lambda j: (0, j)),
                  pl.BlockSpec((1, D), lambda j: (0, j))],
        out_specs=pl.BlockSpec((B, D), lambda j: (0, j)),
        out_shape=jax.ShapeDtypeStruct((B, 3 * D), F32),
        name="adaln_proj",
    )(c, w_ada[0], b_ada[0][None, :])
    ada3 = ada.reshape(B, 3, D)

    w_in_p = jnp.pad(w_in[0], ((0, 0), (0, D_IN_PAD - D_IN))).astype(BF16)
    w_out_b = w_out[0].astype(BF16)
    wgu_p = jnp.pad(w_gate_up[0], ((0, GLR_PAD - GLA_GATE_RANK), (0, 0))).astype(BF16)
    bgu = b_gate_up[0][None, :]
    gg = jnp.tile(gla_norm_gain[0], GLA_HEADS)[None, :]

    inv_freq = ROPE_THETA ** (-jnp.arange(0, RET_DK, 2, dtype=F32) / RET_DK)
    ang = jnp.arange(S, dtype=F32)[:, None] * inv_freq[None, :]
    cos2 = jnp.concatenate([jnp.cos(ang), jnp.cos(ang)], axis=-1)
    sin2 = jnp.concatenate([-jnp.sin(ang), jnp.sin(ang)], axis=-1)

    decay, xi_full, zeta_full, chunk_decay = _retention_tables(T)
    tri, hmask = _gla_tables(T)

    def tile_a(i):
        t = jnp.minimum(i, n_tiles - 1)
        return t // tiles_per_seq, t % tiles_per_seq

    def tile_b(i):
        t = jnp.maximum(i - 1, 0)
        return t // tiles_per_seq, t % tiles_per_seq

    const2 = lambda i: (0, 0)
    const3 = lambda i: (0, 0, 0)
    body = functools.partial(_layer_kernel, ret_chunk_decay=chunk_decay,
                             n_tiles=n_tiles, tiles_per_seq=tiles_per_seq)
    return pl.pallas_call(
        body,
        grid=(n_tiles + 1,),
        in_specs=[
            pl.BlockSpec((1, T, D), lambda i: (*tile_a(i), 0)),
            pl.BlockSpec((1, T, D), lambda i: (*tile_b(i), 0)),
            pl.BlockSpec((B, 3, D), const3),
            pl.BlockSpec((1, D), const2),
            pl.BlockSpec((1, D), const2),
            pl.BlockSpec((1, D_GLA), const2),
            pl.BlockSpec((1, GLA_DK_TOTAL), const2),
            pl.BlockSpec((GLR_PAD, GLA_DK_TOTAL), const2),
            pl.BlockSpec((D, D_IN_PAD), const2),
            pl.BlockSpec((D, D), const2),
            pl.BlockSpec((T, RET_DK), lambda i: (tile_a(i)[1], 0)),
            pl.BlockSpec((T, RET_DK), lambda i: (tile_a(i)[1], 0)),
            pl.BlockSpec((RET_HEADS, RET_CHUNK, RET_CHUNK), const3),
            pl.BlockSpec((T, D_RET), const2),
            pl.BlockSpec((T, D_RET), const2),
            pl.BlockSpec((T, T), const2),
            pl.BlockSpec((GLA_HEADS * GLA_CHUNK, GLA_DK_TOTAL), const2),
        ],
        out_specs=pl.BlockSpec((1, T, D), lambda i: (*tile_b(i), 0)),
        out_shape=jax.ShapeDtypeStruct((B, S, D), F32),
        scratch_shapes=[
            pltpu.VMEM((T, D_MODEL), BF16),
            pltpu.VMEM((T, D_MODEL), BF16),
            pltpu.VMEM((T, 1), F32),
            pltpu.VMEM((RET_HEADS, RET_DK, RET_DV), F32),
            pltpu.VMEM((GLA_DV, GLA_HEADS * GLA_DK), F32),
        ] + _slot_scratch(T) + _slot_scratch(T),
        compiler_params=pltpu.CompilerParams(
            dimension_semantics=("arbitrary",),
            vmem_limit_bytes=VMEM_LIMIT_BYTES),
        name="retention_gla_layer",
    )(x, x, ada3, norm_gain, final_gain[None, :], gg, bgu, wgu_p, w_in_p, w_out_b,
      cos2, sin2, jnp.asarray(decay), jnp.asarray(xi_full), jnp.asarray(zeta_full),
      jnp.asarray(tri, dtype=BF16), jnp.asarray(hmask, dtype=BF16))
```

```python
import functools
from typing import NamedTuple

import numpy as np
import jax
import jax.numpy as jnp
from jax import lax
from jax.experimental import pallas as pl
from jax.experimental.pallas import tpu as pltpu

D_MODEL = 1024
D_RET = 512
D_GLA = 512
RET_HEADS = 4
RET_DK = 128
RET_DV = 128
RET_CHUNK = 128
GLA_HEADS = 4
GLA_DK_TOTAL = 256
GLA_DK = 64
GLA_DV = 128
GLA_CHUNK = 64
GLA_GATE_RANK = 16
GLA_GATE_TAU = 16.0
ROPE_THETA = 10000.0
EPS = 1e-6
D_IN = 4 * D_RET + 2 * GLA_DK_TOTAL + 2 * D_GLA + GLA_GATE_RANK

LANES = 128
MXU_COLS = 256
D_IN_PAD = -(-D_IN // LANES) * LANES
GLR_PAD = D_IN_PAD - (D_IN - GLA_GATE_RANK)

O_RQ, O_RK, O_RV, O_RZ = 0, 512, 1024, 1536
O_GQ, O_GK, O_GV, O_GZ, O_GLR = 2048, 2304, 2560, 3072, 3584

SEQ_TILE = 256
PROJ_LAG, MIX_LAG, OUT_LAG = 1, 2, 3
PIECES_IN_FLIGHT = 2
VMEM_LIMIT_BYTES = 48 * 1024 * 1024

F32 = jnp.float32
BF16 = jnp.bfloat16


class _Slot(NamedTuple):
    hb: object
    q: object
    qx: object
    k: object
    kz: object
    v: object
    rg: object
    qe: object
    ke: object
    kw: object
    gv: object
    gg: object
    ea: object
    mix: object


def _slot_scratch(T):
    shapes = dict(hb=((T, D_MODEL), BF16), mix=((T, D_MODEL), BF16), q=((T, D_RET), BF16),
                  qx=((T, D_RET), BF16), k=((T, D_RET), BF16), kz=((T, D_RET), BF16),
                  v=((T, D_RET), BF16), rg=((T, D_RET), F32),
                  qe=((T, GLA_DK_TOTAL), BF16), ke=((T, GLA_DK_TOTAL), BF16), kw=((T, GLA_DK_TOTAL), BF16),
                  gv=((T, D_GLA), BF16), gg=((T, D_GLA), F32), ea=((8, GLA_DK_TOTAL), F32))
    return [pltpu.VMEM(*shapes[f]) for f in _Slot._fields]


def _split3(a):
    hi = a.astype(BF16)
    r1 = a - hi.astype(F32)
    mid = r1.astype(BF16)
    lo = (r1 - mid.astype(F32)).astype(BF16)
    return hi, mid, lo


def _dot(a, b):
    return jnp.dot(a, b, preferred_element_type=F32)


def _dot_nt(a, b):
    return lax.dot_general(a, b, (((1,), (1,)), ((), ())), preferred_element_type=F32)


def _dot_tn(a, b):
    return lax.dot_general(a, b, (((0,), (0,)), ((), ())), preferred_element_type=F32)


def _silu(z):
    return z * jax.nn.sigmoid(z)


def _ada_kernel(c_ref, w_ref, b_ref, o_ref):
    cs = _silu(c_ref[...])
    w = w_ref[...]
    c_hi = cs.astype(BF16)
    c_lo = (cs - c_hi.astype(F32)).astype(BF16)
    w_hi = w.astype(BF16)
    w_lo = (w - w_hi.astype(F32)).astype(BF16)
    o_ref[...] = _dot(c_hi, w_hi) + _dot(c_lo, w_hi) + _dot(c_hi, w_lo) + b_ref[...]


def _rms_lanes(y):
    return y * lax.rsqrt(jnp.mean(y * y, axis=-1, keepdims=True) + EPS)


def _emit_interleaved(stages, width):
    order = sorted(((j + 0.5) / len(pieces), s, j) for s, pieces in enumerate(stages) for j in range(len(pieces)))
    pending = iter([stages[s][j] for _, s, j in order])
    active = []
    while True:
        while len(active) < width:
            piece = next(pending, None)
            if piece is None:
                break
            active.append(piece())
        if not active:
            return
        for gen in list(active):
            if next(gen, StopIteration) is StopIteration:
                active.remove(gen)


def _layer_kernel(xa_ref, xb_ref, ada_ref, ng_ref, fg_ref, gg_ref, bgu_ref, wgu_ref, win_ref, wout_ref,
                  cos_ref, sin_ref, decay_ref, xi_ref, zeta_ref, tri_ref, hmask_ref,
                  o_ref,
                  ssq_s, r_state, g_state, *slot_refs,
                  ret_chunk_decay, n_tiles, tiles_per_seq):
    i = pl.program_id(0)
    mix_tile = jnp.clip(i - MIX_LAG, 0, n_tiles - 1)
    slots = (_Slot(*slot_refs[:len(_Slot._fields)]), _Slot(*slot_refs[len(_Slot._fields):]))

    @pl.when(i == 0)
    def _():
        for ref in slot_refs:
            ref[...] = jnp.zeros_like(ref)

    @pl.when(mix_tile % tiles_per_seq == 0)
    def _():
        r_state[...] = jnp.zeros_like(r_state)
        g_state[...] = jnp.zeros_like(g_state)

    for parity in range(2):
        @pl.when(i % 2 == parity)
        def _():
            _pipeline_step(xa_ref, xb_ref, ada_ref, ng_ref, fg_ref, gg_ref, bgu_ref, wgu_ref, win_ref,
                           wout_ref, cos_ref, sin_ref, decay_ref, xi_ref, zeta_ref, tri_ref, hmask_ref,
                           o_ref, ssq_s, r_state, g_state,
                           fill=slots[parity], drain=slots[1 - parity],
                           ret_chunk_decay=ret_chunk_decay, n_tiles=n_tiles, tiles_per_seq=tiles_per_seq)


def _pipeline_step(xa_ref, xb_ref, ada_ref, ng_ref, fg_ref, gg_ref, bgu_ref, wgu_ref, win_ref, wout_ref,
                   cos_ref, sin_ref, decay_ref, xi_ref, zeta_ref, tri_ref, hmask_ref,
                   o_ref, ssq_s, r_state, g_state,
                   *, fill, drain, ret_chunk_decay, n_tiles, tiles_per_seq):
    T = xa_ref.shape[1]
    i = pl.program_id(0)
    batch_norm = jnp.clip(i, 0, n_tiles - 1) // tiles_per_seq
    batch_out = jnp.clip(i - OUT_LAG, 0, n_tiles - 1) // tiles_per_seq

    def a_norm(half):
        def piece():
            rows = slice(half * (T // 2), (half + 1) * (T // 2))
            x = xa_ref[0, rows, :]
            shift = ada_ref[batch_norm, 0:1, :]
            scale = ada_ref[batch_norm, 1:2, :]
            h = _rms_lanes(x) * (ng_ref[...] * (1.0 + scale)) + shift
            fill.hb[rows, :] = h.astype(BF16)
            yield
        return piece

    def proj(lo, width):
        return _dot(drain.hb[...], win_ref[:, lo:lo + width])

    def a_rotary(col, out_s, outw_s, w_ref, pair):
        def piece():
            cos = cos_ref[...]
            sin = sin_ref[...]
            p = proj(col + pair * MXU_COLS, MXU_COLS)
            yield
            for j in range(2):
                hh = 2 * pair + j
                sl = slice(hh * RET_DK, (hh + 1) * RET_DK)
                t = p[:, j * RET_DK:(j + 1) * RET_DK]
                r = t * cos + pltpu.roll(t, RET_DK // 2, axis=1) * sin
                out_s[:, sl] = r.astype(BF16)
                outw_s[:, sl] = (r * w_ref[:, sl]).astype(BF16)
        return piece

    def a_plain(col, out_s, half, act=None):
        def piece():
            sl = slice(half * MXU_COLS, (half + 1) * MXU_COLS)
            p = proj(col + half * MXU_COLS, MXU_COLS)
            yield
            if act is not None:
                p = act(p, sl)
            out_s[:, sl] = p.astype(out_s.dtype)
        return piece

    def a_gla_prep():
        gq = proj(O_GQ, GLA_DK_TOTAL) * (GLA_DK ** -0.5)
        gk = proj(O_GK, GLA_DK_TOTAL)
        glr = proj(O_GLR, GLR_PAD)
        yield
        zg = _dot(glr.astype(BF16), wgu_ref[...]) + bgu_ref[...]
        yield
        la = (jnp.minimum(zg, 0.0) - jnp.log1p(jnp.exp(-jnp.abs(zg)))) * (1.0 / GLA_GATE_TAU)
        hi, mid, lo = _split3(la)
        yield
        tri = tri_ref[...]
        b = _dot(tri, hi) + _dot(tri, mid) + _dot(tri, lo)
        yield
        C = GLA_CHUNK
        for c in range(T // C):
            rows = slice(c * C, (c + 1) * C)
            bc = b[rows]
            bl = bc[C - 1:C, :]
            fill.qe[rows, :] = (gq[rows] * jnp.exp(bc)).astype(BF16)
            fill.ke[rows, :] = (gk[rows] * jnp.exp(-bc)).astype(BF16)
            fill.kw[rows, :] = (gk[rows] * jnp.exp(bl - bc)).astype(BF16)
            fill.ea[c:c + 1, :] = jnp.exp(bl)

    def act_silu(p, sl):
        return _silu(p)

    def act_silu_gain(p, sl):
        return _silu(p) * gg_ref[:, sl]

    stage_norm = [a_norm(0), a_norm(1)]
    stage_proj = [
        a_rotary(O_RQ, fill.q, fill.qx, xi_ref, 0), a_rotary(O_RQ, fill.q, fill.qx, xi_ref, 1),
        a_rotary(O_RK, fill.k, fill.kz, zeta_ref, 0), a_rotary(O_RK, fill.k, fill.kz, zeta_ref, 1),
        a_plain(O_RV, fill.v, 0), a_plain(O_RV, fill.v, 1),
        a_plain(O_RZ, fill.rg, 0, act_silu), a_plain(O_RZ, fill.rg, 1, act_silu),
        a_gla_prep,
        a_plain(O_GV, fill.gv, 0), a_plain(O_GV, fill.gv, 1),
        a_plain(O_GZ, fill.gg, 0, act_silu_gain), a_plain(O_GZ, fill.gg, 1, act_silu_gain),
    ]

    def b_ret_block(ci, hh):
        def piece():
            rows = pl.ds(ci * RET_CHUNK, RET_CHUNK)
            sl = slice(hh * RET_DK, (hh + 1) * RET_DK)
            q = drain.q[rows, sl]
            k = drain.k[rows, sl]
            v = drain.v[rows, sl]
            sc = _dot_nt(q, k)
            kv = _dot_tn(drain.kz[rows, sl], v)
            yield
            state = r_state[hh]
            lhs = jnp.concatenate([(sc * decay_ref[hh]).astype(BF16), drain.qx[rows, sl]], axis=1)
            rhs = jnp.concatenate([v, state.astype(BF16)], axis=0)
            y = _dot(lhs, rhs)
            r_state[hh] = state * ret_chunk_decay[hh] + kv
            yield
            fill.mix[rows, sl] = (_rms_lanes(y) * drain.rg[rows, sl]).astype(BF16)
        return piece

    C = GLA_CHUNK
    HC = GLA_HEADS * C

    def b_gla_chunk(c):
        def piece():
            rows = pl.ds(c * C, C)
            head_lanes = hmask_ref[...] > 0
            zero = jnp.zeros((HC, GLA_DK_TOTAL), BF16)

            def stack(ref):
                t = ref[rows, :]
                return jnp.where(head_lanes, jnp.concatenate([t] * GLA_HEADS, axis=0), zero)

            q_st = stack(drain.qe)
            k_st = stack(drain.ke)
            w_st = stack(drain.kw)
            v_st = jnp.concatenate(
                [drain.gv[rows, hh * GLA_DV:(hh + 1) * GLA_DV] for hh in range(GLA_HEADS)], axis=0)
            r_i = lax.broadcasted_iota(jnp.int32, (HC, HC), 0) & (C - 1)
            c_i = lax.broadcasted_iota(jnp.int32, (HC, HC), 1) & (C - 1)
            sc = _dot_nt(q_st, k_st)
            upd = _dot_tn(v_st, w_st)
            yield
            sc = jnp.where(r_i >= c_i, sc, 0.0).astype(BF16)
            state_t = g_state[...]
            y = _dot(sc, v_st) + _dot_nt(q_st, state_t.astype(BF16))
            g_state[...] = state_t * drain.ea[c:c + 1, :] + upd
            yield
            for hh in range(GLA_HEADS):
                sl = slice(hh * GLA_DV, (hh + 1) * GLA_DV)
                fill.mix[rows, D_RET + hh * GLA_DV:D_RET + (hh + 1) * GLA_DV] = (
                    _rms_lanes(y[hh * C:(hh + 1) * C]) * drain.gg[rows, sl]).astype(BF16)
        return piece

    def b_out_proj(n):
        def piece():
            sl = slice(n * MXU_COLS, (n + 1) * MXU_COLS)
            o = _dot(drain.mix[...], wout_ref[:, sl])
            yield
            xn = xb_ref[0, :, sl] + ada_ref[batch_out, 2:3, sl] * o
            o_ref[0, :, sl] = xn
            part = jnp.sum(xn * xn, axis=-1, keepdims=True)
            if n == 0:
                ssq_s[...] = part
            else:
                ssq_s[...] += part
        return piece

    def b_final_norm(half):
        def piece():
            rows = slice(half * (T // 2), (half + 1) * (T // 2))
            scale = lax.rsqrt(ssq_s[rows, :] * (1.0 / D_MODEL) + EPS)
            o_ref[0, rows, :] = o_ref[0, rows, :] * scale * fg_ref[...]
            yield
        return piece

    stage_mix = ([b_ret_block(ci, hh) for ci in range(T // RET_CHUNK) for hh in range(RET_HEADS)]
                 + [b_gla_chunk(c) for c in range(T // C)])
    stage_out = [b_out_proj(n) for n in range(D_MODEL // MXU_COLS)] + [b_final_norm(0), b_final_norm(1)]

    _emit_interleaved([stage_mix, stage_proj, stage_out, stage_norm], width=PIECES_IN_FLIGHT)


def _retention_tables(T):
    H, C = RET_HEADS, RET_CHUNK
    log_g = np.log(1.0 - 2.0 ** (-5.0 - np.arange(H, dtype=np.float64)))
    idx = np.arange(C, dtype=np.float64)
    diff = idx[:, None] - idx[None, :]
    k_scale = RET_DK ** -0.5
    decay = np.where(diff[None] >= 0, np.exp(np.maximum(diff, 0.0)[None] * log_g[:, None, None]), 0.0)
    zeta = np.exp((C - 1.0 - idx)[None, :] * log_g[:, None])
    xi = np.exp((idx + 1.0)[None, :] * log_g[:, None])
    chunk_decay = tuple(float(v) for v in np.exp(C * log_g))

    def full(t):
        return np.tile(np.repeat(t.T, RET_DV, axis=1), (T // C, 1))

    return ((decay * k_scale).astype(np.float32), full(xi).astype(np.float32),
            (full(zeta) * k_scale).astype(np.float32), chunk_decay)


def _gla_tables(T):
    C = GLA_CHUNK
    r = np.arange(T)
    tri = ((r[:, None] // C == r[None, :] // C) & (r[:, None] >= r[None, :])).astype(np.float32)
    row_head = np.arange(GLA_HEADS * C) // C
    lane_head = np.arange(GLA_DK_TOTAL) // GLA_DK
    hmask = (row_head[:, None] == lane_head[None, :]).astype(np.float32)
    return tri, hmask


def kernel(x, c, norm_gain, w_ada, b_ada, w_in, w_gate_up, b_gate_up, gla_norm_gain, w_out, final_gain):
    B, S, D = x.shape
    T = SEQ_TILE
    assert D == D_MODEL and S % T == 0 and T % RET_CHUNK == 0
    tiles_per_seq = S // T
    n_tiles = B * tiles_per_seq

    ada = pl.pallas_call(
        _ada_kernel,
        grid=(3,),
        in_specs=[pl.BlockSpec((B, D), lambda j: (0, 0)),
                  pl.BlockSpec((D, D), lambda j: (0, j)),
                  pl.BlockSpec((1, D), lambda j: (0, j))],
        out_specs=pl.BlockSpec((B, D), lambda j: (0, j)),
        out_shape=jax.ShapeDtypeStruct((B, 3 * D), F32),
        name="adaln_proj",
    )(c, w_ada[0], b_ada[0][None, :])
    ada3 = ada.reshape(B, 3, D)

    w_in_p = jnp.pad(w_in[0], ((0, 0), (0, D_IN_PAD - D_IN))).astype(BF16)
    w_out_b = w_out[0].astype(BF16)
    wgu_p = jnp.pad(w_gate_up[0], ((0, GLR_PAD - GLA_GATE_RANK), (0, 0))).astype(BF16)
    bgu = b_gate_up[0][None, :]
    gg = jnp.tile(gla_norm_gain[0], GLA_HEADS)[None, :]

    inv_freq = ROPE_THETA ** (-jnp.arange(0, RET_DK, 2, dtype=F32) / RET_DK)
    ang = jnp.arange(S, dtype=F32)[:, None] * inv_freq[None, :]
    cos2 = jnp.concatenate([jnp.cos(ang), jnp.cos(ang)], axis=-1)
    sin2 = jnp.concatenate([-jnp.sin(ang), jnp.sin(ang)], axis=-1)

    decay, xi_full, zeta_full, chunk_decay = _retention_tables(T)
    tri, hmask = _gla_tables(T)

    def tile(i, lag):
        t = jnp.clip(i - lag, 0, n_tiles - 1)
        return t // tiles_per_seq, t % tiles_per_seq

    const2 = lambda i: (0, 0)
    const3 = lambda i: (0, 0, 0)
    body = functools.partial(_layer_kernel, ret_chunk_decay=chunk_decay,
                             n_tiles=n_tiles, tiles_per_seq=tiles_per_seq)
    return pl.pallas_call(
        body,
        grid=(n_tiles + OUT_LAG,),
        in_specs=[
            pl.BlockSpec((1, T, D), lambda i: (*tile(i, 0), 0)),
            pl.BlockSpec((1, T, D), lambda i: (*tile(i, OUT_LAG), 0)),
            pl.BlockSpec((B, 3, D), const3),
            pl.BlockSpec((1, D), const2),
            pl.BlockSpec((1, D), const2),
            pl.BlockSpec((1, D_GLA), const2),
            pl.BlockSpec((1, GLA_DK_TOTAL), const2),
            pl.BlockSpec((GLR_PAD, GLA_DK_TOTAL), const2),
            pl.BlockSpec((D, D_IN_PAD), const2),
            pl.BlockSpec((D, D), const2),
            pl.BlockSpec((T, RET_DK), lambda i: (tile(i, PROJ_LAG)[1], 0)),
            pl.BlockSpec((T, RET_DK), lambda i: (tile(i, PROJ_LAG)[1], 0)),
            pl.BlockSpec((RET_HEADS, RET_CHUNK, RET_CHUNK), const3),
            pl.BlockSpec((T, D_RET), const2),
            pl.BlockSpec((T, D_RET), const2),
            pl.BlockSpec((T, T), const2),
            pl.BlockSpec((GLA_HEADS * GLA_CHUNK, GLA_DK_TOTAL), const2),
        ],
        out_specs=pl.BlockSpec((1, T, D), lambda i: (*tile(i, OUT_LAG), 0)),
        out_shape=jax.ShapeDtypeStruct((B, S, D), F32),
        scratch_shapes=[
            pltpu.VMEM((T, 1), F32),
            pltpu.VMEM((RET_HEADS, RET_DK, RET_DV), F32),
            pltpu.VMEM((GLA_DV, GLA_HEADS * GLA_DK), F32),
        ] + _slot_scratch(T) + _slot_scratch(T),
        compiler_params=pltpu.CompilerParams(
            dimension_semantics=("arbitrary",),
            vmem_limit_bytes=VMEM_LIMIT_BYTES),
        name="retention_gla_layer",
    )(x, x, ada3, norm_gain, final_gain[None, :], gg, bgu, wgu_p, w_in_p, w_out_b,
      cos2, sin2, jnp.asarray(decay), jnp.asarray(xi_full), jnp.asarray(zeta_full),
      jnp.asarray(tri, dtype=BF16), jnp.asarray(hmask, dtype=BF16))
```

```python
import functools
from typing import NamedTuple

import numpy as np
import jax
import jax.numpy as jnp
from jax import lax
from jax.experimental import pallas as pl
from jax.experimental.pallas import tpu as pltpu

D_MODEL = 1024
D_RET = 512
D_GLA = 512
RET_HEADS = 4
RET_DK = 128
RET_DV = 128
RET_CHUNK = 128
GLA_HEADS = 4
GLA_DK_TOTAL = 256
GLA_DK = 64
GLA_DV = 128
GLA_CHUNK = 64
GLA_GATE_RANK = 16
GLA_GATE_TAU = 16.0
ROPE_THETA = 10000.0
EPS = 1e-6
D_IN = 4 * D_RET + 2 * GLA_DK_TOTAL + 2 * D_GLA + GLA_GATE_RANK

LANES = 128
MXU_COLS = 256
D_IN_PAD = -(-D_IN // LANES) * LANES
GLR_PAD = D_IN_PAD - (D_IN - GLA_GATE_RANK)

O_RQ, O_RK, O_RV, O_RZ = 0, 512, 1024, 1536
O_GQ, O_GK, O_GV, O_GZ, O_GLR = 2048, 2304, 2560, 3072, 3584

SEQ_TILE = 256
PROJ_LAG, MIX_LAG, OUT_LAG = 1, 2, 3
EMIT_ORDER = "mpopmopmpmpmopmnpmpnompmpomppmopm"
PIECES_IN_FLIGHT = 3
VMEM_LIMIT_BYTES = 48 * 1024 * 1024

F32 = jnp.float32
BF16 = jnp.bfloat16


class _Slot(NamedTuple):
    hb: object
    q: object
    qx: object
    k: object
    kz: object
    v: object
    rg: object
    qe: object
    ke: object
    kw: object
    gv: object
    gg: object
    ea: object
    mix: object


def _slot_scratch(T):
    shapes = dict(hb=((T, D_MODEL), BF16), mix=((T, D_MODEL), BF16), q=((T, D_RET), BF16),
                  qx=((T, D_RET), BF16), k=((T, D_RET), BF16), kz=((T, D_RET), BF16),
                  v=((T, D_RET), BF16), rg=((T, D_RET), F32),
                  qe=((T, GLA_DK_TOTAL), BF16), ke=((T, GLA_DK_TOTAL), BF16), kw=((T, GLA_DK_TOTAL), BF16),
                  gv=((T, D_GLA), BF16), gg=((T, D_GLA), F32), ea=((8, GLA_DK_TOTAL), F32))
    return [pltpu.VMEM(*shapes[f]) for f in _Slot._fields]


def _split2(a):
    hi = a.astype(BF16)
    lo = (a - hi.astype(F32)).astype(BF16)
    return hi, lo


def _dot(a, b):
    return jnp.dot(a, b, preferred_element_type=F32)


def _dot_nt(a, b):
    return lax.dot_general(a, b, (((1,), (1,)), ((), ())), preferred_element_type=F32)


def _dot_tn(a, b):
    return lax.dot_general(a, b, (((0,), (0,)), ((), ())), preferred_element_type=F32)


def _silu(z):
    return z * jax.nn.sigmoid(z)


def _ada_kernel(c_ref, w_ref, b_ref, o_ref):
    cs = _silu(c_ref[...])
    w = w_ref[...]
    c_hi = cs.astype(BF16)
    c_lo = (cs - c_hi.astype(F32)).astype(BF16)
    w_hi = w.astype(BF16)
    w_lo = (w - w_hi.astype(F32)).astype(BF16)
    o_ref[...] = _dot(c_hi, w_hi) + _dot(c_lo, w_hi) + _dot(c_hi, w_lo) + b_ref[...]


def _rms_lanes(y):
    return y * lax.rsqrt(jnp.mean(y * y, axis=-1, keepdims=True) + EPS)


def _merge_stages(stages, order):
    assert sorted(order) == sorted(k for k, pieces in stages.items() for _ in pieces)
    queues = {k: iter(pieces) for k, pieces in stages.items()}
    return [next(queues[k]) for k in order]


def _emit_in_flight(pieces, width):
    pending = iter(pieces)
    active = []
    while True:
        while len(active) < width:
            piece = next(pending, None)
            if piece is None:
                break
            active.append(piece())
        if not active:
            return
        for gen in list(active):
            if next(gen, StopIteration) is StopIteration:
                active.remove(gen)


def _layer_kernel(xa_odd_ref, xa_even_ref, xb_ref, ada_ref, ng_ref, fg_ref, gg_ref, bgu_ref, wgu_ref, win_ref,
                  wout_ref, cos_ref, sin_ref, decay_ref, xi_ref, zeta_ref, tri_ref, hmask_ref, vmask_ref,
                  o_ref,
                  ssq_s, r_state, g_state, *slot_refs,
                  ret_chunk_decay, n_tiles, tiles_per_seq):
    j = pl.program_id(0)
    slots = (_Slot(*slot_refs[:len(_Slot._fields)]), _Slot(*slot_refs[len(_Slot._fields):]))

    @pl.when(j == 0)
    def _():
        for ref in slot_refs + (r_state, g_state):
            ref[...] = jnp.zeros_like(ref)

    T = xa_odd_ref.shape[1]
    pieces = []
    for half, xa_ref in enumerate((xa_odd_ref, xa_even_ref)):
        parity = 1 - half
        pieces += _pipeline_step(
            2 * j - 1 + half, slice(half * T, (half + 1) * T),
            xa_ref, xb_ref, ada_ref, ng_ref, fg_ref, gg_ref, bgu_ref, wgu_ref, win_ref,
            wout_ref, cos_ref, sin_ref, decay_ref, xi_ref, zeta_ref, tri_ref, hmask_ref, vmask_ref,
            o_ref, ssq_s, r_state, g_state,
            fill=slots[parity], drain=slots[1 - parity],
            ret_chunk_decay=ret_chunk_decay, n_tiles=n_tiles, tiles_per_seq=tiles_per_seq)
    _emit_in_flight(pieces, PIECES_IN_FLIGHT)


def _pipeline_step(step, out_rows,
                   xa_ref, xb_ref, ada_ref, ng_ref, fg_ref, gg_ref, bgu_ref, wgu_ref, win_ref, wout_ref,
                   cos_ref, sin_ref, decay_ref, xi_ref, zeta_ref, tri_ref, hmask_ref, vmask_ref,
                   o_ref, ssq_s, r_state, g_state,
                   *, fill, drain, ret_chunk_decay, n_tiles, tiles_per_seq):
    T = xa_ref.shape[1]

    def tile(lag):
        return jnp.clip(step - lag, 0, n_tiles - 1)

    batch_norm = tile(0) // tiles_per_seq
    rope_rows = pl.ds(pl.multiple_of((tile(PROJ_LAG) % tiles_per_seq) * T, T), T)
    new_sequence = tile(MIX_LAG) % tiles_per_seq == 0
    batch_out = tile(OUT_LAG) // tiles_per_seq

    def a_norm(half):
        def piece():
            rows = slice(half * (T // 2), (half + 1) * (T // 2))
            x = xa_ref[0, rows, :]
            shift = ada_ref[batch_norm, 0:1, :]
            scale = ada_ref[batch_norm, 1:2, :]
            h = _rms_lanes(x) * (ng_ref[...] * (1.0 + scale)) + shift
            fill.hb[rows, :] = h.astype(BF16)
            yield
        return piece

    def proj(lo, width):
        return _dot(drain.hb[...], win_ref[:, lo:lo + width])

    def a_rotary(col, out_s, outw_s, w_ref, pair):
        def piece():
            cos = cos_ref[rope_rows, :]
            sin = sin_ref[rope_rows, :]
            p = proj(col + pair * MXU_COLS, MXU_COLS)
            yield
            for j in range(2):
                hh = 2 * pair + j
                sl = slice(hh * RET_DK, (hh + 1) * RET_DK)
                t = p[:, j * RET_DK:(j + 1) * RET_DK]
                r = t * cos + pltpu.roll(t, RET_DK // 2, axis=1) * sin
                out_s[:, sl] = r.astype(BF16)
                outw_s[:, sl] = (r * w_ref[:, sl]).astype(BF16)
        return piece

    def a_plain(col, out_s, half, act=None):
        def piece():
            sl = slice(half * MXU_COLS, (half + 1) * MXU_COLS)
            p = proj(col + half * MXU_COLS, MXU_COLS)
            yield
            if act is not None:
                p = act(p, sl)
            out_s[:, sl] = p.astype(out_s.dtype)
        return piece

    def a_gla_prep():
        gq = proj(O_GQ, GLA_DK_TOTAL) * (GLA_DK ** -0.5)
        gk = proj(O_GK, GLA_DK_TOTAL)
        glr = proj(O_GLR, GLR_PAD)
        yield
        zg = _dot(glr.astype(BF16), wgu_ref[...]) + bgu_ref[...]
        yield
        la = (jnp.minimum(zg, 0.0) - jnp.log(1.0 + jnp.exp(-jnp.abs(zg)))) * (1.0 / GLA_GATE_TAU)
        hi, lo = _split2(la)
        yield
        tri = tri_ref[...]
        b = _dot(tri, hi) + _dot(tri, lo)
        yield
        C = GLA_CHUNK
        for c in range(T // C):
            rows = slice(c * C, (c + 1) * C)
            bc = b[rows]
            bl = bc[C - 1:C, :]
            fill.qe[rows, :] = (gq[rows] * jnp.exp(bc)).astype(BF16)
            fill.ke[rows, :] = (gk[rows] * jnp.exp(-bc)).astype(BF16)
            fill.kw[rows, :] = (gk[rows] * jnp.exp(bl - bc)).astype(BF16)
            fill.ea[c:c + 1, :] = jnp.exp(bl)

    def act_silu(p, sl):
        return _silu(p)

    def act_silu_gain(p, sl):
        return _silu(p) * gg_ref[:, sl]

    stage_norm = [a_norm(0), a_norm(1)]
    stage_proj = [
        a_rotary(O_RQ, fill.q, fill.qx, xi_ref, 0), a_rotary(O_RQ, fill.q, fill.qx, xi_ref, 1),
        a_rotary(O_RK, fill.k, fill.kz, zeta_ref, 0), a_rotary(O_RK, fill.k, fill.kz, zeta_ref, 1),
        a_plain(O_RV, fill.v, 0), a_plain(O_RV, fill.v, 1),
        a_plain(O_RZ, fill.rg, 0, act_silu), a_plain(O_RZ, fill.rg, 1, act_silu),
        a_gla_prep,
        a_plain(O_GV, fill.gv, 0), a_plain(O_GV, fill.gv, 1),
        a_plain(O_GZ, fill.gg, 0, act_silu_gain), a_plain(O_GZ, fill.gg, 1, act_silu_gain),
    ]

    def b_ret_block(ci, hh):
        def piece():
            rows = pl.ds(ci * RET_CHUNK, RET_CHUNK)
            sl = slice(hh * RET_DK, (hh + 1) * RET_DK)
            q = drain.q[rows, sl]
            k = drain.k[rows, sl]
            v = drain.v[rows, sl]
            sc = _dot_nt(q, k)
            kv = _dot_tn(drain.kz[rows, sl], v)
            yield
            state = r_state[hh]
            if ci == 0:
                state = jnp.where(new_sequence, 0.0, state)
            lhs = jnp.concatenate([(sc * decay_ref[hh]).astype(BF16), drain.qx[rows, sl]], axis=1)
            rhs = jnp.concatenate([v, state.astype(BF16)], axis=0)
            y = _dot(lhs, rhs)
            r_state[hh] = state * ret_chunk_decay[hh] + kv
            yield
            fill.mix[rows, sl] = (_rms_lanes(y) * drain.rg[rows, sl]).astype(BF16)
        return piece

    C = GLA_CHUNK
    HC = GLA_HEADS * C

    def b_gla_chunk(c):
        def piece():
            rows = pl.ds(c * C, C)
            head_lanes = hmask_ref[...] > 0
            zero = jnp.zeros((HC, GLA_DK_TOTAL), BF16)

            def stack(ref):
                t = ref[rows, :]
                return jnp.where(head_lanes, jnp.concatenate([t] * GLA_HEADS, axis=0), zero)

            q_st = stack(drain.qe)
            k_st = stack(drain.ke)
            w_st = stack(drain.kw)
            v = drain.gv[rows, :]
            v_st = jnp.concatenate([v[:, hh * GLA_DV:(hh + 1) * GLA_DV] for hh in range(GLA_HEADS)], axis=0)
            sc = _dot_nt(drain.qe[rows, :], k_st)
            upd = _dot_tn(v_st, w_st)
            yield
            r_i = lax.broadcasted_iota(jnp.int32, (C, HC), 0)
            c_i = lax.broadcasted_iota(jnp.int32, (C, HC), 1) & (C - 1)
            sc = jnp.where(r_i >= c_i, sc, 0.0).astype(BF16)
            v_bd = jnp.where(vmask_ref[...] > 0, jnp.concatenate([v] * GLA_HEADS, axis=0),
                             jnp.zeros((HC, D_GLA), BF16))
            inner = _dot(sc, v_bd)
            state_t = g_state[...]
            if c == 0:
                state_t = jnp.where(new_sequence, 0.0, state_t)
            cross = _dot_nt(q_st, state_t.astype(BF16))
            g_state[...] = state_t * drain.ea[c:c + 1, :] + upd
            yield
            for hh in range(GLA_HEADS):
                sl = slice(hh * GLA_DV, (hh + 1) * GLA_DV)
                y = inner[:, sl] + cross[hh * C:(hh + 1) * C]
                fill.mix[rows, D_RET + hh * GLA_DV:D_RET + (hh + 1) * GLA_DV] = (
                    _rms_lanes(y) * drain.gg[rows, sl]).astype(BF16)
        return piece

    def b_out_proj(n):
        def piece():
            sl = slice(n * MXU_COLS, (n + 1) * MXU_COLS)
            o = _dot(drain.mix[...], wout_ref[:, sl])
            yield
            xn = xb_ref[0, out_rows, sl] + ada_ref[batch_out, 2:3, sl] * o
            o_ref[0, out_rows, sl] = xn
            part = jnp.sum(xn * xn, axis=-1, keepdims=True)
            if n == 0:
                ssq_s[out_rows, :] = part
            else:
                ssq_s[out_rows, :] += part
        return piece

    def b_final_norm(half):
        def piece():
            rows = slice(out_rows.start + half * (T // 2), out_rows.start + (half + 1) * (T // 2))
            scale = lax.rsqrt(ssq_s[rows, :] * (1.0 / D_MODEL) + EPS)
            o_ref[0, rows, :] = o_ref[0, rows, :] * scale * fg_ref[...]
            yield
        return piece

    ret_blocks = [b_ret_block(ci, hh) for ci in range(T // RET_CHUNK) for hh in range(RET_HEADS)]
    gla_chunks = [b_gla_chunk(c) for c in range(T // C)]
    stage_mix = ret_blocks + gla_chunks
    stage_out = [b_out_proj(n) for n in range(D_MODEL // MXU_COLS)] + [b_final_norm(0), b_final_norm(1)]

    return _merge_stages(dict(p=stage_proj, m=stage_mix, o=stage_out, n=stage_norm), EMIT_ORDER)


def _retention_tables(T):
    H, C = RET_HEADS, RET_CHUNK
    log_g = np.log(1.0 - 2.0 ** (-5.0 - np.arange(H, dtype=np.float64)))
    idx = np.arange(C, dtype=np.float64)
    diff = idx[:, None] - idx[None, :]
    k_scale = RET_DK ** -0.5
    decay = np.where(diff[None] >= 0, np.exp(np.maximum(diff, 0.0)[None] * log_g[:, None, None]), 0.0)
    zeta = np.exp((C - 1.0 - idx)[None, :] * log_g[:, None])
    xi = np.exp((idx + 1.0)[None, :] * log_g[:, None])
    chunk_decay = tuple(float(v) for v in np.exp(C * log_g))

    def full(t):
        return np.tile(np.repeat(t.T, RET_DV, axis=1), (T // C, 1))

    return ((decay * k_scale).astype(np.float32), full(xi).astype(np.float32),
            (full(zeta) * k_scale).astype(np.float32), chunk_decay)


def _gla_tables(T):
    C = GLA_CHUNK
    r = np.arange(T)
    tri = ((r[:, None] // C == r[None, :] // C) & (r[:, None] >= r[None, :])).astype(np.float32)
    row_head = np.arange(GLA_HEADS * C) // C
    lane_head = np.arange(GLA_DK_TOTAL) // GLA_DK
    hmask = (row_head[:, None] == lane_head[None, :]).astype(np.float32)
    value_head = np.arange(D_GLA) // GLA_DV
    vmask = (row_head[:, None] == value_head[None, :]).astype(np.float32)
    return tri, hmask, vmask


def kernel(x, c, norm_gain, w_ada, b_ada, w_in, w_gate_up, b_gate_up, gla_norm_gain, w_out, final_gain):
    B, S, D = x.shape
    T = SEQ_TILE
    assert D == D_MODEL and S % (2 * T) == 0 and T % RET_CHUNK == 0 and OUT_LAG % 2 == 1
    tiles_per_seq = S // T
    n_tiles = B * tiles_per_seq

    ada = pl.pallas_call(
        _ada_kernel,
        grid=(3,),
        in_specs=[pl.BlockSpec((B, D), lambda j: (0, 0)),
                  pl.BlockSpec((D, D), lambda j: (0, j)),
                  pl.BlockSpec((1, D), lambda j: (0, j))],
        out_specs=pl.BlockSpec((B, D), lambda j: (0, j)),
        out_shape=jax.ShapeDtypeStruct((B, 3 * D), F32),
        name="adaln_proj",
    )(c, w_ada[0], b_ada[0][None, :])
    ada3 = ada.reshape(B, 3, D)

    w_in_p = jnp.pad(w_in[0], ((0, 0), (0, D_IN_PAD - D_IN))).astype(BF16)
    w_out_b = w_out[0].astype(BF16)
    wgu_p = jnp.pad(w_gate_up[0], ((0, GLR_PAD - GLA_GATE_RANK), (0, 0))).astype(BF16)
    bgu = b_gate_up[0][None, :]
    gg = jnp.tile(gla_norm_gain[0], GLA_HEADS)[None, :]

    inv_freq = ROPE_THETA ** (-jnp.arange(0, RET_DK, 2, dtype=F32) / RET_DK)
    ang = jnp.arange(S, dtype=F32)[:, None] * inv_freq[None, :]
    cos2 = jnp.concatenate([jnp.cos(ang), jnp.cos(ang)], axis=-1)
    sin2 = jnp.concatenate([-jnp.sin(ang), jnp.sin(ang)], axis=-1)

    decay, xi_full, zeta_full, chunk_decay = _retention_tables(T)
    tri, hmask, vmask = _gla_tables(T)

    def tile(step, lag):
        t = jnp.clip(step - lag, 0, n_tiles - 1)
        return t // tiles_per_seq, t % tiles_per_seq

    def out_block(j):
        p = jnp.clip(j - (OUT_LAG + 1) // 2, 0, n_tiles // 2 - 1)
        return p // (tiles_per_seq // 2), p % (tiles_per_seq // 2), 0

    def const(rank):
        return dict(index_map=lambda j: (0,) * rank, pipeline_mode=pl.Buffered(1))

    const2, const3 = const(2), const(3)
    body = functools.partial(_layer_kernel, ret_chunk_decay=chunk_decay,
                             n_tiles=n_tiles, tiles_per_seq=tiles_per_seq)
    return pl.pallas_call(
        body,
        grid=(n_tiles // 2 + (OUT_LAG + 1) // 2,),
        in_specs=[
            pl.BlockSpec((1, T, D), lambda j: (*tile(2 * j - 1, 0), 0)),
            pl.BlockSpec((1, T, D), lambda j: (*tile(2 * j, 0), 0)),
            pl.BlockSpec((1, 2 * T, D), out_block),
            pl.BlockSpec((B, 3, D), **const3),
            pl.BlockSpec((1, D), **const2),
            pl.BlockSpec((1, D), **const2),
            pl.BlockSpec((1, D_GLA), **const2),
            pl.BlockSpec((1, GLA_DK_TOTAL), **const2),
            pl.BlockSpec((GLR_PAD, GLA_DK_TOTAL), **const2),
            pl.BlockSpec((D, D_IN_PAD), **const2),
            pl.BlockSpec((D, D), **const2),
            pl.BlockSpec((S, RET_DK), **const2),
            pl.BlockSpec((S, RET_DK), **const2),
            pl.BlockSpec((RET_HEADS, RET_CHUNK, RET_CHUNK), **const3),
            pl.BlockSpec((T, D_RET), **const2),
            pl.BlockSpec((T, D_RET), **const2),
            pl.BlockSpec((T, T), **const2),
            pl.BlockSpec((GLA_HEADS * GLA_CHUNK, GLA_DK_TOTAL), **const2),
            pl.BlockSpec((GLA_HEADS * GLA_CHUNK, D_GLA), **const2),
        ],
        out_specs=pl.BlockSpec((1, 2 * T, D), out_block),
        out_shape=jax.ShapeDtypeStruct((B, S, D), F32),
        scratch_shapes=[
            pltpu.VMEM((2 * T, 1), F32),
            pltpu.VMEM((RET_HEADS, RET_DK, RET_DV), F32),
            pltpu.VMEM((GLA_DV, GLA_HEADS * GLA_DK), F32),
        ] + _slot_scratch(T) + _slot_scratch(T),
        compiler_params=pltpu.CompilerParams(
            dimension_semantics=("arbitrary",),
            vmem_limit_bytes=VMEM_LIMIT_BYTES),
        name="retention_gla_layer",
    )(x, x, x, ada3, norm_gain, final_gain[None, :], gg, bgu, wgu_p, w_in_p, w_out_b,
      cos2, sin2, jnp.asarray(decay), jnp.asarray(xi_full), jnp.asarray(zeta_full),
      jnp.asarray(tri, dtype=BF16), jnp.asarray(hmask, dtype=BF16), jnp.asarray(vmask, dtype=BF16))
```

```python
import functools
from typing import NamedTuple

import numpy as np
import jax
import jax.numpy as jnp
from jax import lax
from jax.experimental import pallas as pl
from jax.experimental.pallas import tpu as pltpu

D_MODEL = 1024
D_RET = 512
D_GLA = 512
RET_HEADS = 4
RET_DK = 128
RET_DV = 128
RET_CHUNK = 128
GLA_HEADS = 4
GLA_DK_TOTAL = 256
GLA_DK = 64
GLA_DV = 128
GLA_CHUNK = 64
GLA_GATE_RANK = 16
GLA_GATE_TAU = 16.0
ROPE_THETA = 10000.0
EPS = 1e-6
D_IN = 4 * D_RET + 2 * GLA_DK_TOTAL + 2 * D_GLA + GLA_GATE_RANK

LANES = 128
MXU_COLS = 256
ADA_COLS = 2 * MXU_COLS

O_RQ, O_RK, O_RV, O_RZ = 0, 512, 1024, 1536
O_GQ, O_GK, O_GV, O_GZ, O_GLR = 2048, 2304, 2560, 3072, 3584

SEQ_TILE = 256
PROJ_LAG, MIX_LAG, OUT_LAG = 1, 2, 3
EMIT_ORDER = "pmopmpmonpmpmopmpmpompmponmpmpomp"
PIECES_IN_FLIGHT = 16
VMEM_LIMIT_BYTES = 48 * 1024 * 1024

F32 = jnp.float32
BF16 = jnp.bfloat16


class _Slot(NamedTuple):
    hb: object
    q: object
    qx: object
    k: object
    kz: object
    v: object
    rg: object
    qe: object
    ke: object
    kw: object
    gv: object
    gg: object
    ea: object
    mix: object


def _slot_scratch(T):
    shapes = dict(hb=((T, D_MODEL), BF16), mix=((T, D_MODEL), BF16), q=((T, D_RET), BF16),
                  qx=((T, D_RET), BF16), k=((T, D_RET), BF16), kz=((T, D_RET), BF16),
                  v=((T, D_RET), BF16), rg=((T, D_RET), F32),
                  qe=((T, GLA_DK_TOTAL), BF16), ke=((T, GLA_DK_TOTAL), BF16), kw=((T, GLA_DK_TOTAL), BF16),
                  gv=((T, D_GLA), BF16), gg=((T, D_GLA), F32), ea=((8, GLA_DK_TOTAL), F32))
    return [pltpu.VMEM(*shapes[f]) for f in _Slot._fields]


def _split2(a):
    hi = a.astype(BF16)
    lo = (a - hi.astype(F32)).astype(BF16)
    return hi, lo


def _dot(a, b):
    return jnp.dot(a, b, preferred_element_type=F32)


def _dot_nt(a, b):
    return lax.dot_general(a, b, (((1,), (1,)), ((), ())), preferred_element_type=F32)


def _dot_tn(a, b):
    return lax.dot_general(a, b, (((0,), (0,)), ((), ())), preferred_element_type=F32)


def _silu(z):
    return z * jax.nn.sigmoid(z)


def _ada_kernel(c_ref, w_ref, b_ref, o_ref):
    cs = _silu(c_ref[...])
    w = w_ref[...]
    c_hi = cs.astype(BF16)
    c_lo = (cs - c_hi.astype(F32)).astype(BF16)
    w_hi = w.astype(BF16)
    w_lo = (w - w_hi.astype(F32)).astype(BF16)
    o_ref[...] = _dot(c_hi, w_hi) + _dot(c_lo, w_hi) + _dot(c_hi, w_lo) + b_ref[...]


def _rms_lanes(y):
    return y * lax.rsqrt(jnp.mean(y * y, axis=-1, keepdims=True) + EPS)


def _merge_stages(stages, order):
    assert sorted(order) == sorted(k for k, pieces in stages.items() for _ in pieces)
    queues = {k: iter(pieces) for k, pieces in stages.items()}
    return [next(queues[k]) for k in order]


def _emit_in_flight(pieces, width):
    pending = list(pieces)
    active, done = [], set()
    while pending or active:
        for piece in list(pending):
            if len(active) == width:
                break
            if all(dep in done for dep in getattr(piece, "after", ())):
                pending.remove(piece)
                active.append((piece, piece()))
        assert active, "a piece waits on one that is never emitted"
        for piece, gen in list(active):
            if next(gen, StopIteration) is StopIteration:
                active.remove((piece, gen))
                done.add(piece)


def _layer_kernel(xa_odd_ref, xa_even_ref, xb_ref, ada_ref, ng_ref, fg_ref, gg_ref, bgu_ref, wgu_ref, win_ref,
                  wout_ref, cos_ref, sin_ref, decay_ref, xi_ref, zeta_ref, tri_ref, hmask_ref, vmask_ref,
                  o_ref,
                  ssq_s, r_state, g_state, *slot_refs,
                  ret_chunk_decay, n_tiles, tiles_per_seq):
    j = pl.program_id(0)
    slots = (_Slot(*slot_refs[:len(_Slot._fields)]), _Slot(*slot_refs[len(_Slot._fields):]))

    @pl.when(j == 0)
    def _():
        for ref in slot_refs + (r_state, g_state):
            ref[...] = jnp.zeros_like(ref)

    T = xa_odd_ref.shape[1]
    for half, xa_ref in enumerate((xa_odd_ref, xa_even_ref)):
        parity = 1 - half
        pieces = _pipeline_step(
            2 * j - 1 + half, slice(half * T, (half + 1) * T),
            xa_ref, xb_ref, ada_ref, ng_ref, fg_ref, gg_ref, bgu_ref, wgu_ref, win_ref,
            wout_ref, cos_ref, sin_ref, decay_ref, xi_ref, zeta_ref, tri_ref, hmask_ref, vmask_ref,
            o_ref, ssq_s, r_state, g_state,
            fill=slots[parity], drain=slots[1 - parity],
            ret_chunk_decay=ret_chunk_decay, n_tiles=n_tiles, tiles_per_seq=tiles_per_seq)
        _emit_in_flight(pieces, PIECES_IN_FLIGHT)


def _pipeline_step(step, out_rows,
                   xa_ref, xb_ref, ada_ref, ng_ref, fg_ref, gg_ref, bgu_ref, wgu_ref, win_ref, wout_ref,
                   cos_ref, sin_ref, decay_ref, xi_ref, zeta_ref, tri_ref, hmask_ref, vmask_ref,
                   o_ref, ssq_s, r_state, g_state,
                   *, fill, drain, ret_chunk_decay, n_tiles, tiles_per_seq):
    T = xa_ref.shape[1]

    def tile(lag):
        return jnp.clip(step - lag, 0, n_tiles - 1)

    batch_norm = tile(0) // tiles_per_seq
    rope_rows = pl.ds(pl.multiple_of((tile(PROJ_LAG) % tiles_per_seq) * T, T), T)
    new_sequence = tile(MIX_LAG) % tiles_per_seq == 0
    batch_out = tile(OUT_LAG) // tiles_per_seq

    def a_norm(half):
        def piece():
            rows = slice(half * (T // 2), (half + 1) * (T // 2))
            x = xa_ref[0, rows, :]
            shift = ada_ref[batch_norm, 0:1, :]
            scale = ada_ref[batch_norm, 1:2, :]
            h = _rms_lanes(x) * (ng_ref[...] * (1.0 + scale)) + shift
            fill.hb[rows, :] = h.astype(BF16)
            yield
        return piece

    def proj(lo, width):
        return _dot(drain.hb[...], win_ref[:, lo:lo + width])

    def a_rotary(col, out_s, outw_s, w_ref, pair):
        def piece():
            cos = cos_ref[rope_rows, :]
            sin = sin_ref[rope_rows, :]
            p = proj(col + pair * MXU_COLS, MXU_COLS)
            yield
            for j in range(2):
                hh = 2 * pair + j
                sl = slice(hh * RET_DK, (hh + 1) * RET_DK)
                t = p[:, j * RET_DK:(j + 1) * RET_DK]
                r = t * cos + pltpu.roll(t, RET_DK // 2, axis=1) * sin
                out_s[:, sl] = r.astype(BF16)
                outw_s[:, sl] = (r * w_ref[:, sl]).astype(BF16)
        return piece

    def a_plain(col, out_s, half, act=None):
        def piece():
            sl = slice(half * MXU_COLS, (half + 1) * MXU_COLS)
            p = proj(col + half * MXU_COLS, MXU_COLS)
            yield
            if act is not None:
                p = act(p, sl)
            out_s[:, sl] = p.astype(out_s.dtype)
        return piece

    def a_gla_prep():
        gq = proj(O_GQ, GLA_DK_TOTAL) * (GLA_DK ** -0.5)
        gk = proj(O_GK, GLA_DK_TOTAL)
        glr = proj(O_GLR, GLA_GATE_RANK)
        yield
        zg = _dot(glr.astype(BF16), wgu_ref[...]) + bgu_ref[...]
        yield
        la = (jnp.minimum(zg, 0.0) - jnp.log(1.0 + jnp.exp(-jnp.abs(zg)))) * (1.0 / GLA_GATE_TAU)
        hi, lo = _split2(la)
        yield
        tri = tri_ref[...]
        b = _dot(tri, hi) + _dot(tri, lo)
        yield
        C = GLA_CHUNK
        for c in range(T // C):
            rows = slice(c * C, (c + 1) * C)
            bc = b[rows]
            bl = bc[C - 1:C, :]
            fill.qe[rows, :] = (gq[rows] * jnp.exp(bc)).astype(BF16)
            fill.ke[rows, :] = (gk[rows] * jnp.exp(-bc)).astype(BF16)
            fill.kw[rows, :] = (gk[rows] * jnp.exp(bl - bc)).astype(BF16)
            fill.ea[c:c + 1, :] = jnp.exp(bl)

    def act_silu(p, sl):
        return _silu(p)

    def act_silu_gain(p, sl):
        return _silu(p) * gg_ref[:, sl]

    stage_norm = [a_norm(0), a_norm(1)]
    stage_proj = [
        a_rotary(O_RQ, fill.q, fill.qx, xi_ref, 0), a_rotary(O_RQ, fill.q, fill.qx, xi_ref, 1),
        a_rotary(O_RK, fill.k, fill.kz, zeta_ref, 0), a_rotary(O_RK, fill.k, fill.kz, zeta_ref, 1),
        a_plain(O_RV, fill.v, 0), a_plain(O_RV, fill.v, 1),
        a_plain(O_RZ, fill.rg, 0, act_silu), a_plain(O_RZ, fill.rg, 1, act_silu),
        a_gla_prep,
        a_plain(O_GV, fill.gv, 0), a_plain(O_GV, fill.gv, 1),
        a_plain(O_GZ, fill.gg, 0, act_silu_gain), a_plain(O_GZ, fill.gg, 1, act_silu_gain),
    ]

    def b_ret_block(ci, hh):
        def piece():
            rows = pl.ds(ci * RET_CHUNK, RET_CHUNK)
            sl = slice(hh * RET_DK, (hh + 1) * RET_DK)
            q = drain.q[rows, sl]
            k = drain.k[rows, sl]
            v = drain.v[rows, sl]
            sc = _dot_nt(q, k)
            kv = _dot_tn(drain.kz[rows, sl], v)
            yield
            state = r_state[hh]
            if ci == 0:
                state = jnp.where(new_sequence, 0.0, state)
            lhs = jnp.concatenate([(sc * decay_ref[hh]).astype(BF16), drain.qx[rows, sl]], axis=1)
            rhs = jnp.concatenate([v, state.astype(BF16)], axis=0)
            y = _dot(lhs, rhs)
            r_state[hh] = state * ret_chunk_decay[hh] + kv
            yield
            fill.mix[rows, sl] = (_rms_lanes(y) * drain.rg[rows, sl]).astype(BF16)
        return piece

    C = GLA_CHUNK
    HC = GLA_HEADS * C

    def b_gla_chunk(c):
        def piece():
            rows = pl.ds(c * C, C)
            head_lanes = hmask_ref[...] > 0
            zero = jnp.zeros((HC, GLA_DK_TOTAL), BF16)

            def stack(ref):
                t = ref[rows, :]
                return jnp.where(head_lanes, jnp.concatenate([t] * GLA_HEADS, axis=0), zero)

            q_st = stack(drain.qe)
            k_st = stack(drain.ke)
            w_st = stack(drain.kw)
            v = drain.gv[rows, :]
            v_st = jnp.concatenate([v[:, hh * GLA_DV:(hh + 1) * GLA_DV] for hh in range(GLA_HEADS)], axis=0)
            sc = _dot_nt(drain.qe[rows, :], k_st)
            upd = _dot_tn(v_st, w_st)
            yield
            r_i = lax.broadcasted_iota(jnp.int32, (C, HC), 0)
            c_i = lax.broadcasted_iota(jnp.int32, (C, HC), 1) & (C - 1)
            sc = jnp.where(r_i >= c_i, sc, 0.0).astype(BF16)
            v_bd = jnp.where(vmask_ref[...] > 0, jnp.concatenate([v] * GLA_HEADS, axis=0),
                             jnp.zeros((HC, D_GLA), BF16))
            inner = _dot(sc, v_bd)
            state_t = g_state[...]
            if c == 0:
                state_t = jnp.where(new_sequence, 0.0, state_t)
            cross = _dot_nt(q_st, state_t.astype(BF16))
            g_state[...] = state_t * drain.ea[c:c + 1, :] + upd
            yield
            for hh in range(GLA_HEADS):
                sl = slice(hh * GLA_DV, (hh + 1) * GLA_DV)
                y = inner[:, sl] + cross[hh * C:(hh + 1) * C]
                fill.mix[rows, D_RET + hh * GLA_DV:D_RET + (hh + 1) * GLA_DV] = (
                    _rms_lanes(y) * drain.gg[rows, sl]).astype(BF16)
        return piece

    def b_out_proj(n):
        def piece():
            sl = slice(n * MXU_COLS, (n + 1) * MXU_COLS)
            o = _dot(drain.mix[...], wout_ref[:, sl])
            yield
            xn = xb_ref[0, out_rows, sl] + ada_ref[batch_out, 2:3, sl] * o
            o_ref[0, out_rows, sl] = xn
            part = jnp.sum(xn * xn, axis=-1, keepdims=True)
            if n == 0:
                ssq_s[out_rows, :] = part
            else:
                ssq_s[out_rows, :] += part
        return piece

    def b_final_norm(half):
        def piece():
            rows = slice(out_rows.start + half * (T // 2), out_rows.start + (half + 1) * (T // 2))
            scale = lax.rsqrt(ssq_s[rows, :] * (1.0 / D_MODEL) + EPS)
            o_ref[0, rows, :] = o_ref[0, rows, :] * scale * fg_ref[...]
            yield
        return piece

    ret_blocks = [b_ret_block(ci, hh) for ci in range(T // RET_CHUNK) for hh in range(RET_HEADS)]
    gla_chunks = [b_gla_chunk(c) for c in range(T // C)]
    stage_mix = ret_blocks + gla_chunks
    out_projs = [b_out_proj(n) for n in range(D_MODEL // MXU_COLS)]
    final_norms = [b_final_norm(0), b_final_norm(1)]
    for piece in final_norms:
        piece.after = out_projs
    stage_out = out_projs + final_norms

    return _merge_stages(dict(p=stage_proj, m=stage_mix, o=stage_out, n=stage_norm), EMIT_ORDER)


def _retention_tables(T):
    H, C = RET_HEADS, RET_CHUNK
    log_g = np.log(1.0 - 2.0 ** (-5.0 - np.arange(H, dtype=np.float64)))
    idx = np.arange(C, dtype=np.float64)
    diff = idx[:, None] - idx[None, :]
    k_scale = RET_DK ** -0.5
    decay = np.where(diff[None] >= 0, np.exp(np.maximum(diff, 0.0)[None] * log_g[:, None, None]), 0.0)
    zeta = np.exp((C - 1.0 - idx)[None, :] * log_g[:, None])
    xi = np.exp((idx + 1.0)[None, :] * log_g[:, None])
    chunk_decay = tuple(float(v) for v in np.exp(C * log_g))

    def full(t):
        return np.tile(np.repeat(t.T, RET_DV, axis=1), (T // C, 1))

    return ((decay * k_scale).astype(np.float32), full(xi).astype(np.float32),
            (full(zeta) * k_scale).astype(np.float32), chunk_decay)


def _gla_tables(T):
    C = GLA_CHUNK
    r = np.arange(T)
    tri = ((r[:, None] // C == r[None, :] // C) & (r[:, None] >= r[None, :])).astype(np.float32)
    row_head = np.arange(GLA_HEADS * C) // C
    lane_head = np.arange(GLA_DK_TOTAL) // GLA_DK
    hmask = (row_head[:, None] == lane_head[None, :]).astype(np.float32)
    value_head = np.arange(D_GLA) // GLA_DV
    vmask = (row_head[:, None] == value_head[None, :]).astype(np.float32)
    return tri, hmask, vmask


def kernel(x, c, norm_gain, w_ada, b_ada, w_in, w_gate_up, b_gate_up, gla_norm_gain, w_out, final_gain):
    B, S, D = x.shape
    T = SEQ_TILE
    assert D == D_MODEL and S % (2 * T) == 0 and T % RET_CHUNK == 0 and OUT_LAG % 2 == 1
    tiles_per_seq = S // T
    n_tiles = B * tiles_per_seq

    ada = pl.pallas_call(
        _ada_kernel,
        grid=(3 * D // ADA_COLS,),
        in_specs=[pl.BlockSpec((B, D), lambda j: (0, 0)),
                  pl.BlockSpec((D, ADA_COLS), lambda j: (0, j)),
                  pl.BlockSpec((1, ADA_COLS), lambda j: (0, j))],
        out_specs=pl.BlockSpec((B, ADA_COLS), lambda j: (0, j)),
        out_shape=jax.ShapeDtypeStruct((B, 3 * D), F32),
        name="adaln_proj",
    )(c, w_ada[0], b_ada[0][None, :])
    ada3 = ada.reshape(B, 3, D)

    w_in_b = w_in[0].astype(BF16)
    w_out_b = w_out[0].astype(BF16)
    wgu_b = w_gate_up[0].astype(BF16)
    bgu = b_gate_up[0][None, :]
    gg = jnp.tile(gla_norm_gain[0], GLA_HEADS)[None, :]

    inv_freq = ROPE_THETA ** (-jnp.arange(0, RET_DK, 2, dtype=F32) / RET_DK)
    ang = jnp.arange(S, dtype=F32)[:, None] * inv_freq[None, :]
    cos2 = jnp.concatenate([jnp.cos(ang), jnp.cos(ang)], axis=-1)
    sin2 = jnp.concatenate([-jnp.sin(ang), jnp.sin(ang)], axis=-1)

    decay, xi_full, zeta_full, chunk_decay = _retention_tables(T)
    tri, hmask, vmask = _gla_tables(T)

    def tile(step, lag):
        t = jnp.clip(step - lag, 0, n_tiles - 1)
        return t // tiles_per_seq, t % tiles_per_seq

    def out_block(j):
        p = jnp.clip(j - (OUT_LAG + 1) // 2, 0, n_tiles // 2 - 1)
        return p // (tiles_per_seq // 2), p % (tiles_per_seq // 2), 0

    def const(rank):
        return dict(index_map=lambda j: (0,) * rank, pipeline_mode=pl.Buffered(1))

    const2, const3 = const(2), const(3)
    body = functools.partial(_layer_kernel, ret_chunk_decay=chunk_decay,
                             n_tiles=n_tiles, tiles_per_seq=tiles_per_seq)
    return pl.pallas_call(
        body,
        grid=(n_tiles // 2 + (OUT_LAG + 1) // 2,),
        in_specs=[
            pl.BlockSpec((1, T, D), lambda j: (*tile(2 * j - 1, 0), 0)),
            pl.BlockSpec((1, T, D), lambda j: (*tile(2 * j, 0), 0)),
            pl.BlockSpec((1, 2 * T, D), out_block),
            pl.BlockSpec((B, 3, D), **const3),
            pl.BlockSpec((1, D), **const2),
            pl.BlockSpec((1, D), **const2),
            pl.BlockSpec((1, D_GLA), **const2),
            pl.BlockSpec((1, GLA_DK_TOTAL), **const2),
            pl.BlockSpec((GLA_GATE_RANK, GLA_DK_TOTAL), **const2),
            pl.BlockSpec((D, D_IN), **const2),
            pl.BlockSpec((D, D), **const2),
            pl.BlockSpec((S, RET_DK), **const2),
            pl.BlockSpec((S, RET_DK), **const2),
            pl.BlockSpec((RET_HEADS, RET_CHUNK, RET_CHUNK), **const3),
            pl.BlockSpec((T, D_RET), **const2),
            pl.BlockSpec((T, D_RET), **const2),
            pl.BlockSpec((T, T), **const2),
            pl.BlockSpec((GLA_HEADS * GLA_CHUNK, GLA_DK_TOTAL), **const2),
            pl.BlockSpec((GLA_HEADS * GLA_CHUNK, D_GLA), **const2),
        ],
        out_specs=pl.BlockSpec((1, 2 * T, D), out_block),
        out_shape=jax.ShapeDtypeStruct((B, S, D), F32),
        scratch_shapes=[
            pltpu.VMEM((2 * T, 1), F32),
            pltpu.VMEM((RET_HEADS, RET_DK, RET_DV), F32),
            pltpu.VMEM((GLA_DV, GLA_HEADS * GLA_DK), F32),
        ] + _slot_scratch(T) + _slot_scratch(T),
        compiler_params=pltpu.CompilerParams(
            dimension_semantics=("arbitrary",),
            vmem_limit_bytes=VMEM_LIMIT_BYTES),
        name="retention_gla_layer",
    )(x, x, x, ada3, norm_gain, final_gain[None, :], gg, bgu, wgu_b, w_in_b, w_out_b,
      cos2, sin2, jnp.asarray(decay), jnp.asarray(xi_full), jnp.asarray(zeta_full),
      jnp.asarray(tri, dtype=BF16), jnp.asarray(hmask, dtype=BF16), jnp.asarray(vmask, dtype=BF16))
```

```python
import functools
from typing import NamedTuple

import numpy as np
import jax
import jax.numpy as jnp
from jax import lax
from jax.experimental import pallas as pl
from jax.experimental.pallas import tpu as pltpu

D_MODEL = 1024
D_RET = 512
D_GLA = 512
RET_HEADS = 4
RET_DK = 128
RET_DV = 128
RET_CHUNK = 128
GLA_HEADS = 4
GLA_DK_TOTAL = 256
GLA_DK = 64
GLA_DV = 128
GLA_CHUNK = 64
GLA_GATE_RANK = 16
GLA_GATE_TAU = 16.0
ROPE_THETA = 10000.0
EPS = 1e-6
D_IN = 4 * D_RET + 2 * GLA_DK_TOTAL + 2 * D_GLA + GLA_GATE_RANK

MXU_COLS = 256
ADA_COLS = 2 * MXU_COLS

O_RQ, O_RK, O_RV, O_RZ = 0, 512, 1024, 1536
O_GQ, O_GK, O_GV, O_GZ, O_GLR = 2048, 2304, 2560, 3072, 3584

SEQ_TILE = 256
PROJ_LAG, MIX_LAG, OUT_LAG = 1, 2, 3
EMIT_ORDER = "pmopmpmonpmpmopmpmpompmponmpmpomp"
STEPS_PER_GRID_STEP = 4
PIECES_IN_FLIGHT = 16
VMEM_LIMIT_BYTES = 56 * 1024 * 1024

F32 = jnp.float32
BF16 = jnp.bfloat16


class _Slot(NamedTuple):
    hb: object
    q: object
    qx: object
    k: object
    kz: object
    v: object
    rg: object
    qe: object
    ke: object
    kw: object
    gv: object
    gg: object
    ea: object
    mix: object


def _slot_scratch(T):
    shapes = dict(hb=((T, D_MODEL), BF16), mix=((T, D_MODEL), BF16), q=((T, D_RET), BF16),
                  qx=((T, D_RET), BF16), k=((T, D_RET), BF16), kz=((T, D_RET), BF16),
                  v=((T, D_RET), BF16), rg=((T, D_RET), F32),
                  qe=((T, GLA_DK_TOTAL), BF16), ke=((T, GLA_DK_TOTAL), BF16), kw=((T, GLA_DK_TOTAL), BF16),
                  gv=((T, D_GLA), BF16), gg=((T, D_GLA), F32), ea=((8, GLA_DK_TOTAL), F32))
    return [pltpu.VMEM(*shapes[f]) for f in _Slot._fields]


def _split2(a):
    hi = a.astype(BF16)
    lo = (a - hi.astype(F32)).astype(BF16)
    return hi, lo


def _dot(a, b):
    return jnp.dot(a, b, preferred_element_type=F32)


def _dot_nt(a, b):
    return lax.dot_general(a, b, (((1,), (1,)), ((), ())), preferred_element_type=F32)


def _dot_tn(a, b):
    return lax.dot_general(a, b, (((0,), (0,)), ((), ())), preferred_element_type=F32)


def _silu(z):
    return z * jax.nn.sigmoid(z)


def _ada_kernel(c_ref, w_ref, b_ref, o_ref):
    cs = _silu(c_ref[...])
    w = w_ref[...]
    c_hi = cs.astype(BF16)
    c_lo = (cs - c_hi.astype(F32)).astype(BF16)
    w_hi = w.astype(BF16)
    w_lo = (w - w_hi.astype(F32)).astype(BF16)
    o_ref[...] = _dot(c_hi, w_hi) + _dot(c_lo, w_hi) + _dot(c_hi, w_lo) + b_ref[...]


def _rms_lanes(y):
    return y * lax.rsqrt(jnp.mean(y * y, axis=-1, keepdims=True) + EPS)


def _merge_stages(stages, order):
    assert sorted(order) == sorted(k for k, pieces in stages.items() for _ in pieces)
    queues = {k: iter(pieces) for k, pieces in stages.items()}
    return [next(queues[k]) for k in order]


def _emit_in_flight(pieces, width):
    pending = list(pieces)
    active, done = [], set()
    while pending or active:
        for piece in list(pending):
            if len(active) == width:
                break
            if all(dep in done for dep in getattr(piece, "after", ())):
                pending.remove(piece)
                active.append((piece, piece()))
        assert active, "a piece waits on one that is never emitted"
        for piece, gen in list(active):
            if next(gen, StopIteration) is StopIteration:
                active.remove((piece, gen))
                done.add(piece)


def _layer_kernel(*refs, ret_chunk_decay, n_tiles, tiles_per_seq):
    xa_refs = refs[:STEPS_PER_GRID_STEP]
    (xb_ref, ada_ref, ng_ref, fg_ref, gg_ref, bgu_ref, wgu_ref, wglr_ref, win_ref,
     wout_ref, cos_ref, sin_ref, decay_ref, xi_ref, zeta_ref, tri_ref, hmask_ref, vmask_ref,
     o_ref,
     ssq_s, r_state, g_state, *slot_refs) = refs[STEPS_PER_GRID_STEP:]
    slot_refs = tuple(slot_refs)
    j = pl.program_id(0)
    slots = (_Slot(*slot_refs[:len(_Slot._fields)]), _Slot(*slot_refs[len(_Slot._fields):]))

    @pl.when(j == 0)
    def _():
        for ref in slot_refs + (r_state, g_state):
            ref[...] = jnp.zeros_like(ref)

    T = xa_refs[0].shape[1]
    for k, xa_ref in enumerate(xa_refs):
        parity = (k + 1) % 2
        pieces = _pipeline_step(
            STEPS_PER_GRID_STEP * j - 1 + k, slice(k * T, (k + 1) * T),
            xa_ref, xb_ref, ada_ref, ng_ref, fg_ref, gg_ref, bgu_ref, wgu_ref, wglr_ref, win_ref,
            wout_ref, cos_ref, sin_ref, decay_ref, xi_ref, zeta_ref, tri_ref, hmask_ref, vmask_ref,
            o_ref, ssq_s, r_state, g_state,
            fill=slots[parity], drain=slots[1 - parity],
            ret_chunk_decay=ret_chunk_decay, n_tiles=n_tiles, tiles_per_seq=tiles_per_seq)
        _emit_in_flight(pieces, PIECES_IN_FLIGHT)


def _pipeline_step(step, out_rows,
                   xa_ref, xb_ref, ada_ref, ng_ref, fg_ref, gg_ref, bgu_ref, wgu_ref, wglr_ref, win_ref, wout_ref,
                   cos_ref, sin_ref, decay_ref, xi_ref, zeta_ref, tri_ref, hmask_ref, vmask_ref,
                   o_ref, ssq_s, r_state, g_state,
                   *, fill, drain, ret_chunk_decay, n_tiles, tiles_per_seq):
    T = xa_ref.shape[1]

    def tile(lag):
        return jnp.clip(step - lag, 0, n_tiles - 1)

    batch_norm = tile(0) // tiles_per_seq
    rope_rows = pl.ds(pl.multiple_of((tile(PROJ_LAG) % tiles_per_seq) * T, T), T)
    new_sequence = tile(MIX_LAG) % tiles_per_seq == 0
    batch_out = tile(OUT_LAG) // tiles_per_seq

    def a_norm(half):
        def piece():
            rows = slice(half * (T // 2), (half + 1) * (T // 2))
            x = xa_ref[0, rows, :]
            shift = ada_ref[batch_norm, 0:1, :]
            scale = ada_ref[batch_norm, 1:2, :]
            h = _rms_lanes(x) * (ng_ref[...] * (1.0 + scale)) + shift
            fill.hb[rows, :] = h.astype(BF16)
            yield
        return piece

    def proj(lo, width):
        return _dot(drain.hb[...], win_ref[:, lo:lo + width])

    def a_rotary(col, out_s, outw_s, w_ref, pair):
        def piece():
            cos = cos_ref[rope_rows, :]
            sin = sin_ref[rope_rows, :]
            p = proj(col + pair * MXU_COLS, MXU_COLS)
            yield
            for j in range(2):
                hh = 2 * pair + j
                sl = slice(hh * RET_DK, (hh + 1) * RET_DK)
                t = p[:, j * RET_DK:(j + 1) * RET_DK]
                r = t * cos + pltpu.roll(t, RET_DK // 2, axis=1) * sin
                out_s[:, sl] = r.astype(BF16)
                outw_s[:, sl] = (r * w_ref[:, sl]).astype(BF16)
        return piece

    def a_plain(col, out_s, half, act=None):
        def piece():
            sl = slice(half * MXU_COLS, (half + 1) * MXU_COLS)
            p = proj(col + half * MXU_COLS, MXU_COLS)
            yield
            if act is not None:
                p = act(p, sl)
            out_s[:, sl] = p.astype(out_s.dtype)
        return piece

    def a_gla_prep():
        gq = proj(O_GQ, GLA_DK_TOTAL) * (GLA_DK ** -0.5)
        gk = proj(O_GK, GLA_DK_TOTAL)
        glr = _dot(drain.hb[...], wglr_ref[...])
        yield
        zg = _dot(glr.astype(BF16), wgu_ref[...]) + bgu_ref[...]
        yield
        la = (jnp.minimum(zg, 0.0) - jnp.log(1.0 + jnp.exp(-jnp.abs(zg)))) * (1.0 / GLA_GATE_TAU)
        hi, lo = _split2(la)
        yield
        tri = tri_ref[...]
        b = _dot(tri, hi) + _dot(tri, lo)
        yield
        C = GLA_CHUNK
        for c in range(T // C):
            rows = slice(c * C, (c + 1) * C)
            bc = b[rows]
            bl = bc[C - 1:C, :]
            fill.qe[rows, :] = (gq[rows] * jnp.exp(bc)).astype(BF16)
            fill.ke[rows, :] = (gk[rows] * jnp.exp(-bc)).astype(BF16)
            fill.kw[rows, :] = (gk[rows] * jnp.exp(bl - bc)).astype(BF16)
            fill.ea[c:c + 1, :] = jnp.exp(bl)

    def act_silu(p, sl):
        return _silu(p)

    def act_silu_gain(p, sl):
        return _silu(p) * gg_ref[:, sl]

    stage_norm = [a_norm(0), a_norm(1)]
    stage_proj = [
        a_rotary(O_RQ, fill.q, fill.qx, xi_ref, 0), a_rotary(O_RQ, fill.q, fill.qx, xi_ref, 1),
        a_rotary(O_RK, fill.k, fill.kz, zeta_ref, 0), a_rotary(O_RK, fill.k, fill.kz, zeta_ref, 1),
        a_plain(O_RV, fill.v, 0), a_plain(O_RV, fill.v, 1),
        a_plain(O_RZ, fill.rg, 0, act_silu), a_plain(O_RZ, fill.rg, 1, act_silu),
        a_gla_prep,
        a_plain(O_GV, fill.gv, 0), a_plain(O_GV, fill.gv, 1),
        a_plain(O_GZ, fill.gg, 0, act_silu_gain), a_plain(O_GZ, fill.gg, 1, act_silu_gain),
    ]

    def b_ret_block(ci, hh):
        def piece():
            rows = pl.ds(ci * RET_CHUNK, RET_CHUNK)
            sl = slice(hh * RET_DK, (hh + 1) * RET_DK)
            q = drain.q[rows, sl]
            k = drain.k[rows, sl]
            v = drain.v[rows, sl]
            sc = _dot_nt(q, k)
            kv = _dot_tn(drain.kz[rows, sl], v)
            yield
            state = r_state[hh]
            if ci == 0:
                state = jnp.where(new_sequence, 0.0, state)
            lhs = jnp.concatenate([(sc * decay_ref[hh]).astype(BF16), drain.qx[rows, sl]], axis=1)
            rhs = jnp.concatenate([v, state.astype(BF16)], axis=0)
            y = _dot(lhs, rhs)
            r_state[hh] = state * ret_chunk_decay[hh] + kv
            yield
            fill.mix[rows, sl] = (_rms_lanes(y) * drain.rg[rows, sl]).astype(BF16)
        return piece

    C = GLA_CHUNK
    HC = GLA_HEADS * C

    def b_gla_chunk(c):
        def piece():
            rows = pl.ds(c * C, C)
            head_lanes = hmask_ref[...] > 0
            zero = jnp.zeros((HC, GLA_DK_TOTAL), BF16)

            def stack(ref):
                t = ref[rows, :]
                return jnp.where(head_lanes, jnp.concatenate([t] * GLA_HEADS, axis=0), zero)

            q_st = stack(drain.qe)
            k_st = stack(drain.ke)
            w_st = stack(drain.kw)
            v = drain.gv[rows, :]
            v_st = jnp.concatenate([v[:, hh * GLA_DV:(hh + 1) * GLA_DV] for hh in range(GLA_HEADS)], axis=0)
            sc = _dot_nt(drain.qe[rows, :], k_st)
            upd = _dot_tn(v_st, w_st)
            yield
            r_i = lax.broadcasted_iota(jnp.int32, (C, HC), 0)
            c_i = lax.broadcasted_iota(jnp.int32, (C, HC), 1) & (C - 1)
            sc = jnp.where(r_i >= c_i, sc, 0.0).astype(BF16)
            v_bd = jnp.where(vmask_ref[...] > 0, jnp.concatenate([v] * GLA_HEADS, axis=0),
                             jnp.zeros((HC, D_GLA), BF16))
            inner = _dot(sc, v_bd)
            state_t = g_state[...]
            if c == 0:
                state_t = jnp.where(new_sequence, 0.0, state_t)
            cross = _dot_nt(q_st, state_t.astype(BF16))
            g_state[...] = state_t * drain.ea[c:c + 1, :] + upd
            yield
            for hh in range(GLA_HEADS):
                sl = slice(hh * GLA_DV, (hh + 1) * GLA_DV)
                y = inner[:, sl] + cross[hh * C:(hh + 1) * C]
                fill.mix[rows, D_RET + hh * GLA_DV:D_RET + (hh + 1) * GLA_DV] = (
                    _rms_lanes(y) * drain.gg[rows, sl]).astype(BF16)
        return piece

    def b_out_proj(n):
        def piece():
            sl = slice(n * MXU_COLS, (n + 1) * MXU_COLS)
            o = _dot(drain.mix[...], wout_ref[:, sl])
            yield
            xn = xb_ref[0, out_rows, sl] + ada_ref[batch_out, 2:3, sl] * o
            o_ref[0, out_rows, sl] = xn
            part = jnp.sum(xn * xn, axis=-1, keepdims=True)
            if n == 0:
                ssq_s[out_rows, :] = part
            else:
                ssq_s[out_rows, :] += part
        return piece

    def b_final_norm(half):
        def piece():
            rows = slice(out_rows.start + half * (T // 2), out_rows.start + (half + 1) * (T // 2))
            scale = lax.rsqrt(ssq_s[rows, :] * (1.0 / D_MODEL) + EPS)
            o_ref[0, rows, :] = o_ref[0, rows, :] * scale * fg_ref[...]
            yield
        return piece

    ret_blocks = [b_ret_block(ci, hh) for ci in range(T // RET_CHUNK) for hh in range(RET_HEADS)]
    gla_chunks = [b_gla_chunk(c) for c in range(T // C)]
    stage_mix = ret_blocks + gla_chunks
    out_projs = [b_out_proj(n) for n in range(D_MODEL // MXU_COLS)]
    final_norms = [b_final_norm(0), b_final_norm(1)]
    for piece in final_norms:
        piece.after = out_projs
    stage_out = out_projs + final_norms

    return _merge_stages(dict(p=stage_proj, m=stage_mix, o=stage_out, n=stage_norm), EMIT_ORDER)


def _retention_tables(T):
    H, C = RET_HEADS, RET_CHUNK
    log_g = np.log(1.0 - 2.0 ** (-5.0 - np.arange(H, dtype=np.float64)))
    idx = np.arange(C, dtype=np.float64)
    diff = idx[:, None] - idx[None, :]
    k_scale = RET_DK ** -0.5
    decay = np.where(diff[None] >= 0, np.exp(np.maximum(diff, 0.0)[None] * log_g[:, None, None]), 0.0)
    zeta = np.exp((C - 1.0 - idx)[None, :] * log_g[:, None])
    xi = np.exp((idx + 1.0)[None, :] * log_g[:, None])
    chunk_decay = tuple(float(v) for v in np.exp(C * log_g))

    def full(t):
        return np.tile(np.repeat(t.T, RET_DV, axis=1), (T // C, 1))

    return ((decay * k_scale).astype(np.float32), full(xi).astype(np.float32),
            (full(zeta) * k_scale).astype(np.float32), chunk_decay)


def _gla_tables(T):
    C = GLA_CHUNK
    r = np.arange(T)
    tri = ((r[:, None] // C == r[None, :] // C) & (r[:, None] >= r[None, :])).astype(np.float32)
    row_head = np.arange(GLA_HEADS * C) // C
    lane_head = np.arange(GLA_DK_TOTAL) // GLA_DK
    hmask = (row_head[:, None] == lane_head[None, :]).astype(np.float32)
    value_head = np.arange(D_GLA) // GLA_DV
    vmask = (row_head[:, None] == value_head[None, :]).astype(np.float32)
    return tri, hmask, vmask


def kernel(x, c, norm_gain, w_ada, b_ada, w_in, w_gate_up, b_gate_up, gla_norm_gain, w_out, final_gain):
    B, S, D = x.shape
    T = SEQ_TILE
    K = STEPS_PER_GRID_STEP
    assert D == D_MODEL and S % (K * T) == 0 and T % RET_CHUNK == 0 and (OUT_LAG + 1) % K == 0 and K % 2 == 0
    tiles_per_seq = S // T
    n_tiles = B * tiles_per_seq

    ada = pl.pallas_call(
        _ada_kernel,
        grid=(3 * D // ADA_COLS,),
        in_specs=[pl.BlockSpec((B, D), lambda j: (0, 0)),
                  pl.BlockSpec((D, ADA_COLS), lambda j: (0, j)),
                  pl.BlockSpec((1, ADA_COLS), lambda j: (0, j))],
        out_specs=pl.BlockSpec((B, ADA_COLS), lambda j: (0, j)),
        out_shape=jax.ShapeDtypeStruct((B, 3 * D), F32),
        name="adaln_proj",
    )(c, w_ada[0], b_ada[0][None, :])
    ada3 = ada.reshape(B, 3, D)

    w_in_b = w_in[0, :, :O_GLR].astype(BF16)
    w_glr_b = w_in[0, :, O_GLR:].astype(BF16)
    w_out_b = w_out[0].astype(BF16)
    wgu_b = w_gate_up[0].astype(BF16)
    bgu = b_gate_up[0][None, :]
    gg = jnp.tile(gla_norm_gain[0], GLA_HEADS)[None, :]

    inv_freq = ROPE_THETA ** (-jnp.arange(0, RET_DK, 2, dtype=F32) / RET_DK)
    ang = jnp.arange(S, dtype=F32)[:, None] * inv_freq[None, :]
    cos2 = jnp.concatenate([jnp.cos(ang), jnp.cos(ang)], axis=-1)
    sin2 = jnp.concatenate([-jnp.sin(ang), jnp.sin(ang)], axis=-1)

    decay, xi_full, zeta_full, chunk_decay = _retention_tables(T)
    tri, hmask, vmask = _gla_tables(T)

    def tile(step, lag):
        t = jnp.clip(step - lag, 0, n_tiles - 1)
        return t // tiles_per_seq, t % tiles_per_seq

    def out_block(j):
        p = jnp.clip(j - (OUT_LAG + 1) // K, 0, n_tiles // K - 1)
        return p // (tiles_per_seq // K), p % (tiles_per_seq // K), 0

    def const(rank):
        return dict(index_map=lambda j: (0,) * rank, pipeline_mode=pl.Buffered(1))

    const2, const3 = const(2), const(3)
    body = functools.partial(_layer_kernel, ret_chunk_decay=chunk_decay,
                             n_tiles=n_tiles, tiles_per_seq=tiles_per_seq)
    return pl.pallas_call(
        body,
        grid=((n_tiles + OUT_LAG + 1) // K,),
        in_specs=[
            pl.BlockSpec((1, T, D), functools.partial(lambda k, j: (*tile(K * j - 1 + k, 0), 0), k))
            for k in range(K)
        ] + [
            pl.BlockSpec((1, K * T, D), out_block),
            pl.BlockSpec((B, 3, D), **const3),
            pl.BlockSpec((1, D), **const2),
            pl.BlockSpec((1, D), **const2),
            pl.BlockSpec((1, D_GLA), **const2),
            pl.BlockSpec((1, GLA_DK_TOTAL), **const2),
            pl.BlockSpec((GLA_GATE_RANK, GLA_DK_TOTAL), **const2),
            pl.BlockSpec((D, GLA_GATE_RANK), **const2),
            pl.BlockSpec((D, O_GLR), **const2),
            pl.BlockSpec((D, D), **const2),
            pl.BlockSpec((S, RET_DK), **const2),
            pl.BlockSpec((S, RET_DK), **const2),
            pl.BlockSpec((RET_HEADS, RET_CHUNK, RET_CHUNK), **const3),
            pl.BlockSpec((T, D_RET), **const2),
            pl.BlockSpec((T, D_RET), **const2),
            pl.BlockSpec((T, T), **const2),
            pl.BlockSpec((GLA_HEADS * GLA_CHUNK, GLA_DK_TOTAL), **const2),
            pl.BlockSpec((GLA_HEADS * GLA_CHUNK, D_GLA), **const2),
        ],
        out_specs=pl.BlockSpec((1, K * T, D), out_block),
        out_shape=jax.ShapeDtypeStruct((B, S, D), F32),
        scratch_shapes=[
            pltpu.VMEM((K * T, 1), F32),
            pltpu.VMEM((RET_HEADS, RET_DK, RET_DV), F32),
            pltpu.VMEM((GLA_DV, GLA_HEADS * GLA_DK), F32),
        ] + _slot_scratch(T) + _slot_scratch(T),
        compiler_params=pltpu.CompilerParams(
            dimension_semantics=("arbitrary",),
            vmem_limit_bytes=VMEM_LIMIT_BYTES),
        name="retention_gla_layer",
    )(*([x] * K), x, ada3, norm_gain, final_gain[None, :], gg, bgu, wgu_b, w_glr_b, w_in_b, w_out_b,
      cos2, sin2, jnp.asarray(decay), jnp.asarray(xi_full), jnp.asarray(zeta_full),
      jnp.asarray(tri, dtype=BF16), jnp.asarray(hmask, dtype=BF16), jnp.asarray(vmask, dtype=BF16))
```

```python
import functools
from typing import NamedTuple

import numpy as np
import jax
import jax.numpy as jnp
from jax import lax
from jax.experimental import pallas as pl
from jax.experimental.pallas import tpu as pltpu

D_MODEL = 1024
D_RET = 512
D_GLA = 512
RET_HEADS = 4
RET_DK = 128
RET_DV = 128
RET_CHUNK = 128
GLA_HEADS = 4
GLA_DK_TOTAL = 256
GLA_DK = 64
GLA_DV = 128
GLA_CHUNK = 64
GLA_GATE_RANK = 16
GLA_GATE_TAU = 16.0
ROPE_THETA = 10000.0
EPS = 1e-6
D_IN = 4 * D_RET + 2 * GLA_DK_TOTAL + 2 * D_GLA + GLA_GATE_RANK

MXU_COLS = 256
ADA_COLS = 2 * MXU_COLS
W_IN_CAST_COLS = 2 * MXU_COLS

O_RQ, O_RK, O_RV, O_RZ = 0, 512, 1024, 1536
O_GQ, O_GK, O_GV, O_GZ, O_GLR = 2048, 2304, 2560, 3072, 3584

SEQ_TILE = 256
PROJ_LAG, MIX_LAG, OUT_LAG = 1, 2, 3
EMIT_ORDER = "pmopmpmonpmpmopmpmpompmponmpmpomp"
STEPS_PER_GRID_STEP = 2
PIECES_IN_FLIGHT = 16
VMEM_LIMIT_BYTES = 56 * 1024 * 1024

F32 = jnp.float32
BF16 = jnp.bfloat16


class _Slot(NamedTuple):
    hb: object
    q: object
    qx: object
    k: object
    kz: object
    v: object
    rg: object
    qe: object
    ke: object
    kw: object
    gv: object
    gg: object
    ea: object
    mix: object


def _slot_scratch(T):
    shapes = dict(hb=((T, D_MODEL), BF16), mix=((T, D_MODEL), BF16), q=((T, D_RET), BF16),
                  qx=((T, D_RET), BF16), k=((T, D_RET), BF16), kz=((T, D_RET), BF16),
                  v=((T, D_RET), BF16), rg=((T, D_RET), F32),
                  qe=((T, GLA_DK_TOTAL), BF16), ke=((T, GLA_DK_TOTAL), BF16), kw=((T, GLA_DK_TOTAL), BF16),
                  gv=((T, D_GLA), BF16), gg=((T, D_GLA), F32), ea=((8, GLA_DK_TOTAL), F32))
    return [pltpu.VMEM(*shapes[f]) for f in _Slot._fields]


def _split2(a):
    hi = a.astype(BF16)
    lo = (a - hi.astype(F32)).astype(BF16)
    return hi, lo


def _dot(a, b):
    return jnp.dot(a, b, preferred_element_type=F32)


def _dot_nt(a, b):
    return lax.dot_general(a, b, (((1,), (1,)), ((), ())), preferred_element_type=F32)


def _dot_tn(a, b):
    return lax.dot_general(a, b, (((0,), (0,)), ((), ())), preferred_element_type=F32)


def _silu(z):
    return z * jax.nn.sigmoid(z)


def _ada_kernel(c_ref, w_ref, b_ref, o_ref):
    cs = _silu(c_ref[...])
    w = w_ref[...]
    c_hi = cs.astype(BF16)
    c_lo = (cs - c_hi.astype(F32)).astype(BF16)
    w_hi = w.astype(BF16)
    w_lo = (w - w_hi.astype(F32)).astype(BF16)
    o_ref[...] = _dot(c_hi, w_hi) + _dot(c_lo, w_hi) + _dot(c_hi, w_lo) + b_ref[...]


def _rms_lanes(y):
    return y * lax.rsqrt(jnp.mean(y * y, axis=-1, keepdims=True) + EPS)


def _merge_stages(stages, order):
    assert sorted(order) == sorted(k for k, pieces in stages.items() for _ in pieces)
    queues = {k: iter(pieces) for k, pieces in stages.items()}
    return [next(queues[k]) for k in order]


def _emit_in_flight(pieces, width):
    pending = list(pieces)
    active, done = [], set()
    while pending or active:
        for piece in list(pending):
            if len(active) == width:
                break
            if all(dep in done for dep in getattr(piece, "after", ())):
                pending.remove(piece)
                active.append((piece, piece()))
        assert active, "a piece waits on one that is never emitted"
        for piece, gen in list(active):
            if next(gen, StopIteration) is StopIteration:
                active.remove((piece, gen))
                done.add(piece)


def _layer_kernel(*refs, ret_chunk_decay, n_tiles, tiles_per_seq):
    xa_refs = refs[:STEPS_PER_GRID_STEP]
    (xb_ref, ada_ref, ng_ref, fg_ref, gg_ref, bgu_ref, wgu_ref, win_ref,
     wout_ref, cos_ref, sin_ref, decay_ref, xi_ref, zeta_ref, tri_ref, hmask_ref, vmask_ref,
     o_ref,
     win_s, ssq_s, r_state, g_state, *slot_refs) = refs[STEPS_PER_GRID_STEP:]
    slot_refs = tuple(slot_refs)
    j = pl.program_id(0)
    slots = (_Slot(*slot_refs[:len(_Slot._fields)]), _Slot(*slot_refs[len(_Slot._fields):]))

    @pl.when(j == 0)
    def _():
        for ref in slot_refs + (r_state, g_state):
            ref[...] = jnp.zeros_like(ref)
        for lo in range(0, D_IN, W_IN_CAST_COLS):
            hi = min(lo + W_IN_CAST_COLS, D_IN)
            win_s[:, lo:hi] = win_ref[:, lo:hi].astype(BF16)

    T = xa_refs[0].shape[1]
    for k, xa_ref in enumerate(xa_refs):
        parity = (k + 1) % 2
        pieces = _pipeline_step(
            STEPS_PER_GRID_STEP * j - 1 + k, slice(k * T, (k + 1) * T),
            xa_ref, xb_ref, ada_ref, ng_ref, fg_ref, gg_ref, bgu_ref, wgu_ref, win_s,
            wout_ref, cos_ref, sin_ref, decay_ref, xi_ref, zeta_ref, tri_ref, hmask_ref, vmask_ref,
            o_ref, ssq_s, r_state, g_state,
            fill=slots[parity], drain=slots[1 - parity],
            ret_chunk_decay=ret_chunk_decay, n_tiles=n_tiles, tiles_per_seq=tiles_per_seq)
        _emit_in_flight(pieces, PIECES_IN_FLIGHT)


def _pipeline_step(step, out_rows,
                   xa_ref, xb_ref, ada_ref, ng_ref, fg_ref, gg_ref, bgu_ref, wgu_ref, win_ref, wout_ref,
                   cos_ref, sin_ref, decay_ref, xi_ref, zeta_ref, tri_ref, hmask_ref, vmask_ref,
                   o_ref, ssq_s, r_state, g_state,
                   *, fill, drain, ret_chunk_decay, n_tiles, tiles_per_seq):
    T = xa_ref.shape[1]

    def tile(lag):
        return jnp.clip(step - lag, 0, n_tiles - 1)

    batch_norm = tile(0) // tiles_per_seq
    rope_rows = pl.ds(pl.multiple_of((tile(PROJ_LAG) % tiles_per_seq) * T, T), T)
    new_sequence = tile(MIX_LAG) % tiles_per_seq == 0
    batch_out = tile(OUT_LAG) // tiles_per_seq

    def a_norm(half):
        def piece():
            rows = slice(half * (T // 2), (half + 1) * (T // 2))
            x = xa_ref[0, rows, :]
            shift = ada_ref[batch_norm, 0:1, :]
            scale = ada_ref[batch_norm, 1:2, :]
            h = _rms_lanes(x) * (ng_ref[...] * (1.0 + scale)) + shift
            fill.hb[rows, :] = h.astype(BF16)
            yield
        return piece

    def proj(lo, width):
        return _dot(drain.hb[...], win_ref[:, lo:lo + width])

    def a_rotary(col, out_s, outw_s, w_ref, pair):
        def piece():
            cos = cos_ref[rope_rows, :]
            sin = sin_ref[rope_rows, :]
            p = proj(col + pair * MXU_COLS, MXU_COLS)
            yield
            for j in range(2):
                hh = 2 * pair + j
                sl = slice(hh * RET_DK, (hh + 1) * RET_DK)
                t = p[:, j * RET_DK:(j + 1) * RET_DK]
                r = t * cos + pltpu.roll(t, RET_DK // 2, axis=1) * sin
                out_s[:, sl] = r.astype(BF16)
                outw_s[:, sl] = (r * w_ref[:, sl]).astype(BF16)
        return piece

    def a_plain(col, out_s, half, act=None):
        def piece():
            sl = slice(half * MXU_COLS, (half + 1) * MXU_COLS)
            p = proj(col + half * MXU_COLS, MXU_COLS)
            yield
            if act is not None:
                p = act(p, sl)
            out_s[:, sl] = p.astype(out_s.dtype)
        return piece

    def a_gla_prep():
        gq = proj(O_GQ, GLA_DK_TOTAL) * (GLA_DK ** -0.5)
        gk = proj(O_GK, GLA_DK_TOTAL)
        glr = proj(O_GLR, GLA_GATE_RANK)
        yield
        zg = _dot(glr.astype(BF16), wgu_ref[...]) + bgu_ref[...]
        yield
        la = (jnp.minimum(zg, 0.0) - jnp.log(1.0 + jnp.exp(-jnp.abs(zg)))) * (1.0 / GLA_GATE_TAU)
        hi, lo = _split2(la)
        yield
        tri = tri_ref[...]
        b = _dot(tri, hi) + _dot(tri, lo)
        yield
        C = GLA_CHUNK
        for c in range(T // C):
            rows = slice(c * C, (c + 1) * C)
            bc = b[rows]
            bl = bc[C - 1:C, :]
            fill.qe[rows, :] = (gq[rows] * jnp.exp(bc)).astype(BF16)
            fill.ke[rows, :] = (gk[rows] * jnp.exp(-bc)).astype(BF16)
            fill.kw[rows, :] = (gk[rows] * jnp.exp(bl - bc)).astype(BF16)
            fill.ea[c:c + 1, :] = jnp.exp(bl)

    def act_silu(p, sl):
        return _silu(p)

    def act_silu_gain(p, sl):
        return _silu(p) * gg_ref[:, sl]

    stage_norm = [a_norm(0), a_norm(1)]
    stage_proj = [
        a_rotary(O_RQ, fill.q, fill.qx, xi_ref, 0), a_rotary(O_RQ, fill.q, fill.qx, xi_ref, 1),
        a_rotary(O_RK, fill.k, fill.kz, zeta_ref, 0), a_rotary(O_RK, fill.k, fill.kz, zeta_ref, 1),
        a_plain(O_RV, fill.v, 0), a_plain(O_RV, fill.v, 1),
        a_plain(O_RZ, fill.rg, 0, act_silu), a_plain(O_RZ, fill.rg, 1, act_silu),
        a_gla_prep,
        a_plain(O_GV, fill.gv, 0), a_plain(O_GV, fill.gv, 1),
        a_plain(O_GZ, fill.gg, 0, act_silu_gain), a_plain(O_GZ, fill.gg, 1, act_silu_gain),
    ]

    def b_ret_block(ci, hh):
        def piece():
            rows = pl.ds(ci * RET_CHUNK, RET_CHUNK)
            sl = slice(hh * RET_DK, (hh + 1) * RET_DK)
            q = drain.q[rows, sl]
            k = drain.k[rows, sl]
            v = drain.v[rows, sl]
            sc = _dot_nt(q, k)
            kv = _dot_tn(drain.kz[rows, sl], v)
            yield
            state = r_state[hh]
            if ci == 0:
                state = jnp.where(new_sequence, 0.0, state)
            lhs = jnp.concatenate([(sc * decay_ref[hh]).astype(BF16), drain.qx[rows, sl]], axis=1)
            rhs = jnp.concatenate([v, state.astype(BF16)], axis=0)
            y = _dot(lhs, rhs)
            r_state[hh] = state * ret_chunk_decay[hh] + kv
            yield
            fill.mix[rows, sl] = (_rms_lanes(y) * drain.rg[rows, sl]).astype(BF16)
        return piece

    C = GLA_CHUNK
    HC = GLA_HEADS * C

    def b_gla_chunk(c):
        def piece():
            rows = pl.ds(c * C, C)
            head_lanes = hmask_ref[...] > 0
            zero = jnp.zeros((HC, GLA_DK_TOTAL), BF16)

            def stack(ref):
                t = ref[rows, :]
                return jnp.where(head_lanes, jnp.concatenate([t] * GLA_HEADS, axis=0), zero)

            q_st = stack(drain.qe)
            k_st = stack(drain.ke)
            w_st = stack(drain.kw)
            v = drain.gv[rows, :]
            v_st = jnp.concatenate([v[:, hh * GLA_DV:(hh + 1) * GLA_DV] for hh in range(GLA_HEADS)], axis=0)
            sc = _dot_nt(drain.qe[rows, :], k_st)
            upd = _dot_tn(v_st, w_st)
            yield
            r_i = lax.broadcasted_iota(jnp.int32, (C, HC), 0)
            c_i = lax.broadcasted_iota(jnp.int32, (C, HC), 1) & (C - 1)
            sc = jnp.where(r_i >= c_i, sc, 0.0).astype(BF16)
            v_bd = jnp.where(vmask_ref[...] > 0, jnp.concatenate([v] * GLA_HEADS, axis=0),
                             jnp.zeros((HC, D_GLA), BF16))
            inner = _dot(sc, v_bd)
            state_t = g_state[...]
            if c == 0:
                state_t = jnp.where(new_sequence, 0.0, state_t)
            cross = _dot_nt(q_st, state_t.astype(BF16))
            g_state[...] = state_t * drain.ea[c:c + 1, :] + upd
            yield
            for hh in range(GLA_HEADS):
                sl = slice(hh * GLA_DV, (hh + 1) * GLA_DV)
                y = inner[:, sl] + cross[hh * C:(hh + 1) * C]
                fill.mix[rows, D_RET + hh * GLA_DV:D_RET + (hh + 1) * GLA_DV] = (
                    _rms_lanes(y) * drain.gg[rows, sl]).astype(BF16)
        return piece

    def b_out_proj(n):
        def piece():
            sl = slice(n * MXU_COLS, (n + 1) * MXU_COLS)
            o = _dot(drain.mix[...], wout_ref[:, sl])
            yield
            xn = xb_ref[0, out_rows, sl] + ada_ref[batch_out, 2:3, sl] * o
            o_ref[0, out_rows, sl] = xn
            part = jnp.sum(xn * xn, axis=-1, keepdims=True)
            if n == 0:
                ssq_s[out_rows, :] = part
            else:
                ssq_s[out_rows, :] += part
        return piece

    def b_final_norm(half):
        def piece():
            rows = slice(out_rows.start + half * (T // 2), out_rows.start + (half + 1) * (T // 2))
            scale = lax.rsqrt(ssq_s[rows, :] * (1.0 / D_MODEL) + EPS)
            o_ref[0, rows, :] = o_ref[0, rows, :] * scale * fg_ref[...]
            yield
        return piece

    ret_blocks = [b_ret_block(ci, hh) for ci in range(T // RET_CHUNK) for hh in range(RET_HEADS)]
    gla_chunks = [b_gla_chunk(c) for c in range(T // C)]
    stage_mix = ret_blocks + gla_chunks
    out_projs = [b_out_proj(n) for n in range(D_MODEL // MXU_COLS)]
    final_norms = [b_final_norm(0), b_final_norm(1)]
    for piece in final_norms:
        piece.after = out_projs
    stage_out = out_projs + final_norms

    return _merge_stages(dict(p=stage_proj, m=stage_mix, o=stage_out, n=stage_norm), EMIT_ORDER)


def _retention_tables(T):
    H, C = RET_HEADS, RET_CHUNK
    log_g = np.log(1.0 - 2.0 ** (-5.0 - np.arange(H, dtype=np.float64)))
    idx = np.arange(C, dtype=np.float64)
    diff = idx[:, None] - idx[None, :]
    k_scale = RET_DK ** -0.5
    decay = np.where(diff[None] >= 0, np.exp(np.maximum(diff, 0.0)[None] * log_g[:, None, None]), 0.0)
    zeta = np.exp((C - 1.0 - idx)[None, :] * log_g[:, None])
    xi = np.exp((idx + 1.0)[None, :] * log_g[:, None])
    chunk_decay = tuple(float(v) for v in np.exp(C * log_g))

    def full(t):
        return np.tile(np.repeat(t.T, RET_DV, axis=1), (T // C, 1))

    return ((decay * k_scale).astype(np.float32), full(xi).astype(np.float32),
            (full(zeta) * k_scale).astype(np.float32), chunk_decay)


def _gla_tables(T):
    C = GLA_CHUNK
    r = np.arange(T)
    tri = ((r[:, None] // C == r[None, :] // C) & (r[:, None] >= r[None, :])).astype(np.float32)
    row_head = np.arange(GLA_HEADS * C) // C
    lane_head = np.arange(GLA_DK_TOTAL) // GLA_DK
    hmask = (row_head[:, None] == lane_head[None, :]).astype(np.float32)
    value_head = np.arange(D_GLA) // GLA_DV
    vmask = (row_head[:, None] == value_head[None, :]).astype(np.float32)
    return tri, hmask, vmask


def kernel(x, c, norm_gain, w_ada, b_ada, w_in, w_gate_up, b_gate_up, gla_norm_gain, w_out, final_gain):
    B, S, D = x.shape
    T = SEQ_TILE
    K = STEPS_PER_GRID_STEP
    assert D == D_MODEL and S % (K * T) == 0 and T % RET_CHUNK == 0 and (OUT_LAG + 1) % K == 0 and K % 2 == 0
    tiles_per_seq = S // T
    n_tiles = B * tiles_per_seq

    ada = pl.pallas_call(
        _ada_kernel,
        grid=(3 * D // ADA_COLS,),
        in_specs=[pl.BlockSpec((B, D), lambda j: (0, 0)),
                  pl.BlockSpec((D, ADA_COLS), lambda j: (0, j)),
                  pl.BlockSpec((1, ADA_COLS), lambda j: (0, j))],
        out_specs=pl.BlockSpec((B, ADA_COLS), lambda j: (0, j)),
        out_shape=jax.ShapeDtypeStruct((B, 3 * D), F32),
        name="adaln_proj",
    )(c, w_ada[0], b_ada[0][None, :])
    ada3 = ada.reshape(B, 3, D)

    w_out_b = w_out[0].astype(BF16)
    wgu_b = w_gate_up[0].astype(BF16)
    bgu = b_gate_up[0][None, :]
    gg = jnp.tile(gla_norm_gain[0], GLA_HEADS)[None, :]

    inv_freq = ROPE_THETA ** (-jnp.arange(0, RET_DK, 2, dtype=F32) / RET_DK)
    ang = jnp.arange(S, dtype=F32)[:, None] * inv_freq[None, :]
    cos2 = jnp.concatenate([jnp.cos(ang), jnp.cos(ang)], axis=-1)
    sin2 = jnp.concatenate([-jnp.sin(ang), jnp.sin(ang)], axis=-1)

    decay, xi_full, zeta_full, chunk_decay = _retention_tables(T)
    tri, hmask, vmask = _gla_tables(T)

    def tile(step, lag):
        t = jnp.clip(step - lag, 0, n_tiles - 1)
        return t // tiles_per_seq, t % tiles_per_seq

    def out_block(j):
        p = jnp.clip(j - (OUT_LAG + 1) // K, 0, n_tiles // K - 1)
        return p // (tiles_per_seq // K), p % (tiles_per_seq // K), 0

    def const(rank):
        return dict(index_map=lambda j: (0,) * rank, pipeline_mode=pl.Buffered(1))

    const2, const3 = const(2), const(3)
    body = functools.partial(_layer_kernel, ret_chunk_decay=chunk_decay,
                             n_tiles=n_tiles, tiles_per_seq=tiles_per_seq)
    return pl.pallas_call(
        body,
        grid=((n_tiles + OUT_LAG + 1) // K,),
        in_specs=[
            pl.BlockSpec((1, T, D), functools.partial(lambda k, j: (*tile(K * j - 1 + k, 0), 0), k))
            for k in range(K)
        ] + [
            pl.BlockSpec((1, K * T, D), out_block),
            pl.BlockSpec((B, 3, D), **const3),
            pl.BlockSpec((1, D), **const2),
            pl.BlockSpec((1, D), **const2),
            pl.BlockSpec((1, D_GLA), **const2),
            pl.BlockSpec((1, GLA_DK_TOTAL), **const2),
            pl.BlockSpec((GLA_GATE_RANK, GLA_DK_TOTAL), **const2),
            pl.BlockSpec((D, D_IN), **const2),
            pl.BlockSpec((D, D), **const2),
            pl.BlockSpec((S, RET_DK), **const2),
            pl.BlockSpec((S, RET_DK), **const2),
            pl.BlockSpec((RET_HEADS, RET_CHUNK, RET_CHUNK), **const3),
            pl.BlockSpec((T, D_RET), **const2),
            pl.BlockSpec((T, D_RET), **const2),
            pl.BlockSpec((T, T), **const2),
            pl.BlockSpec((GLA_HEADS * GLA_CHUNK, GLA_DK_TOTAL), **const2),
            pl.BlockSpec((GLA_HEADS * GLA_CHUNK, D_GLA), **const2),
        ],
        out_specs=pl.BlockSpec((1, K * T, D), out_block),
        out_shape=jax.ShapeDtypeStruct((B, S, D), F32),
        scratch_shapes=[
            pltpu.VMEM((D, D_IN), BF16),
            pltpu.VMEM((K * T, 1), F32),
            pltpu.VMEM((RET_HEADS, RET_DK, RET_DV), F32),
            pltpu.VMEM((GLA_DV, GLA_HEADS * GLA_DK), F32),
        ] + _slot_scratch(T) + _slot_scratch(T),
        compiler_params=pltpu.CompilerParams(
            dimension_semantics=("arbitrary",),
            vmem_limit_bytes=VMEM_LIMIT_BYTES),
        name="retention_gla_layer",
    )(*([x] * K), x, ada3, norm_gain, final_gain[None, :], gg, bgu, wgu_b, w_in[0], w_out_b,
      cos2, sin2, jnp.asarray(decay), jnp.asarray(xi_full), jnp.asarray(zeta_full),
      jnp.asarray(tri, dtype=BF16), jnp.asarray(hmask, dtype=BF16), jnp.asarray(vmask, dtype=BF16))
```

```python
import functools
from typing import NamedTuple

import numpy as np
import jax
import jax.numpy as jnp
from jax import lax
from jax.experimental import pallas as pl
from jax.experimental.pallas import tpu as pltpu

D_MODEL = 1024
D_RET = 512
D_GLA = 512
RET_HEADS = 4
RET_DK = 128
RET_DV = 128
RET_CHUNK = 128
GLA_HEADS = 4
GLA_DK_TOTAL = 256
GLA_DK = 64
GLA_DV = 128
GLA_CHUNK = 64
GLA_GATE_RANK = 16
GLA_GATE_TAU = 16.0
ROPE_THETA = 10000.0
EPS = 1e-6
D_IN = 4 * D_RET + 2 * GLA_DK_TOTAL + 2 * D_GLA + GLA_GATE_RANK

MXU_COLS = 256
ADA_COLS = 2 * MXU_COLS

O_RQ, O_RK, O_RV, O_RZ = 0, 512, 1024, 1536
O_GQ, O_GK, O_GV, O_GZ, O_GLR = 2048, 2304, 2560, 3072, 3584

SEQ_TILE = 256
PROJ_LAG, MIX_LAG, OUT_LAG = 1, 2, 3
EMIT_ORDER = "pmopmpmonpmpmopmpmpompmponmpmpomp"
STEPS_PER_GRID_STEP = 2
PIECES_IN_FLIGHT = 16
VMEM_LIMIT_BYTES = 48 * 1024 * 1024

F32 = jnp.float32
BF16 = jnp.bfloat16


class _Slot(NamedTuple):
    hb: object
    q: object
    qx: object
    k: object
    kz: object
    v: object
    rg: object
    qe: object
    ke: object
    kw: object
    gv: object
    gg: object
    ea: object
    mix: object


def _slot_scratch(T):
    shapes = dict(hb=((T, D_MODEL), BF16), mix=((T, D_MODEL), BF16), q=((T, D_RET), BF16),
                  qx=((T, D_RET), BF16), k=((T, D_RET), BF16), kz=((T, D_RET), BF16),
                  v=((T, D_RET), BF16), rg=((T, D_RET), BF16),
                  qe=((T, GLA_DK_TOTAL), BF16), ke=((T, GLA_DK_TOTAL), BF16), kw=((T, GLA_DK_TOTAL), BF16),
                  gv=((T, D_GLA), BF16), gg=((T, D_GLA), BF16), ea=((8, GLA_DK_TOTAL), F32))
    return [pltpu.VMEM(*shapes[f]) for f in _Slot._fields]


def _split2(a):
    hi = a.astype(BF16)
    lo = (a - hi.astype(F32)).astype(BF16)
    return hi, lo


def _dot(a, b):
    return jnp.dot(a, b, preferred_element_type=F32)


def _dot_nt(a, b):
    return lax.dot_general(a, b, (((1,), (1,)), ((), ())), preferred_element_type=F32)


def _dot_tn(a, b):
    return lax.dot_general(a, b, (((0,), (0,)), ((), ())), preferred_element_type=F32)


def _silu(z):
    return z * jax.nn.sigmoid(z)


def _ada_kernel(c_ref, w_ref, b_ref, o_ref):
    cs = _silu(c_ref[...])
    w = w_ref[...]
    c_hi = cs.astype(BF16)
    c_lo = (cs - c_hi.astype(F32)).astype(BF16)
    w_hi = w.astype(BF16)
    w_lo = (w - w_hi.astype(F32)).astype(BF16)
    n = cs.shape[0]
    both = _dot(jnp.concatenate([c_hi, c_lo], axis=0), w_hi)
    o_ref[...] = both[:n] + both[n:] + _dot(c_hi, w_lo) + b_ref[...]


def _rms_lanes(y):
    return y * lax.rsqrt(jnp.mean(y * y, axis=-1, keepdims=True) + EPS)


def _merge_stages(stages, order):
    assert sorted(order) == sorted(k for k, pieces in stages.items() for _ in pieces)
    queues = {k: iter(pieces) for k, pieces in stages.items()}
    return [next(queues[k]) for k in order]


def _emit_in_flight(pieces, width):
    pending = list(pieces)
    active, done = [], set()
    while pending or active:
        for piece in list(pending):
            if len(active) == width:
                break
            if all(dep in done for dep in getattr(piece, "after", ())):
                pending.remove(piece)
                active.append((piece, piece()))
        assert active, "a piece waits on one that is never emitted"
        for piece, gen in list(active):
            if next(gen, StopIteration) is StopIteration:
                active.remove((piece, gen))
                done.add(piece)


def _layer_kernel(*refs, ret_chunk_decay, n_tiles, tiles_per_seq):
    xa_refs = refs[:STEPS_PER_GRID_STEP]
    (xb_ref, ada_ref, ng_ref, fg_ref, gg_ref, bgu_ref, wgu_ref, win_ref,
     wout_ref, cos_ref, sin_ref, decay_ref, xi_ref, zeta_ref, tri_ref, hmask_ref, vmask_ref,
     o_ref,
     ssq_s, r_state, g_state, *slot_refs) = refs[STEPS_PER_GRID_STEP:]
    slot_refs = tuple(slot_refs)
    j = pl.program_id(0)
    slots = (_Slot(*slot_refs[:len(_Slot._fields)]), _Slot(*slot_refs[len(_Slot._fields):]))

    @pl.when(j == 0)
    def _():
        for ref in slot_refs + (r_state, g_state):
            ref[...] = jnp.zeros_like(ref)

    T = xa_refs[0].shape[1]
    for k, xa_ref in enumerate(xa_refs):
        parity = (k + 1) % 2
        pieces = _pipeline_step(
            STEPS_PER_GRID_STEP * j - 1 + k, slice(k * T, (k + 1) * T),
            xa_ref, xb_ref, ada_ref, ng_ref, fg_ref, gg_ref, bgu_ref, wgu_ref, win_ref,
            wout_ref, cos_ref, sin_ref, decay_ref, xi_ref, zeta_ref, tri_ref, hmask_ref, vmask_ref,
            o_ref, ssq_s, r_state, g_state,
            fill=slots[parity], drain=slots[1 - parity],
            ret_chunk_decay=ret_chunk_decay, n_tiles=n_tiles, tiles_per_seq=tiles_per_seq)
        _emit_in_flight(pieces, PIECES_IN_FLIGHT)


def _pipeline_step(step, out_rows,
                   xa_ref, xb_ref, ada_ref, ng_ref, fg_ref, gg_ref, bgu_ref, wgu_ref, win_ref, wout_ref,
                   cos_ref, sin_ref, decay_ref, xi_ref, zeta_ref, tri_ref, hmask_ref, vmask_ref,
                   o_ref, ssq_s, r_state, g_state,
                   *, fill, drain, ret_chunk_decay, n_tiles, tiles_per_seq):
    T = xa_ref.shape[1]

    def tile(lag):
        return jnp.clip(step - lag, 0, n_tiles - 1)

    batch_norm = tile(0) // tiles_per_seq
    rope_rows = pl.ds(pl.multiple_of((tile(PROJ_LAG) % tiles_per_seq) * T, T), T)
    new_sequence = tile(MIX_LAG) % tiles_per_seq == 0
    batch_out = tile(OUT_LAG) // tiles_per_seq

    def a_norm(half):
        def piece():
            rows = slice(half * (T // 2), (half + 1) * (T // 2))
            x = xa_ref[0, rows, :]
            shift = ada_ref[batch_norm, 0:1, :]
            scale = ada_ref[batch_norm, 1:2, :]
            h = _rms_lanes(x) * (ng_ref[...] * (1.0 + scale)) + shift
            fill.hb[rows, :] = h.astype(BF16)
            yield
        return piece

    def proj(lo, width):
        return _dot(drain.hb[...], win_ref[:, lo:lo + width])

    def a_rotary(col, out_s, outw_s, w_ref, pair):
        def piece():
            cos = cos_ref[rope_rows, :]
            sin = sin_ref[rope_rows, :]
            p = proj(col + pair * MXU_COLS, MXU_COLS)
            yield
            for j in range(2):
                hh = 2 * pair + j
                sl = slice(hh * RET_DK, (hh + 1) * RET_DK)
                t = p[:, j * RET_DK:(j + 1) * RET_DK]
                r = t * cos + pltpu.roll(t, RET_DK // 2, axis=1) * sin
                out_s[:, sl] = r.astype(BF16)
                outw_s[:, sl] = (r * w_ref[:, sl]).astype(BF16)
        return piece

    def a_plain(col, out_s, half, act=None):
        def piece():
            sl = slice(half * MXU_COLS, (half + 1) * MXU_COLS)
            p = proj(col + half * MXU_COLS, MXU_COLS)
            yield
            if act is not None:
                p = act(p, sl)
            out_s[:, sl] = p.astype(out_s.dtype)
        return piece

    def a_gla_prep():
        gq = proj(O_GQ, GLA_DK_TOTAL) * (GLA_DK ** -0.5)
        gk = proj(O_GK, GLA_DK_TOTAL)
        glr = proj(O_GLR, GLA_GATE_RANK)
        yield
        zg = _dot(glr.astype(BF16), wgu_ref[...]) + bgu_ref[...]
        yield
        la = (jnp.minimum(zg, 0.0) - jnp.log(1.0 + jnp.exp(-jnp.abs(zg)))) * (1.0 / GLA_GATE_TAU)
        hi, lo = _split2(la)
        yield
        tri = tri_ref[...]
        b = _dot(tri, hi) + _dot(tri, lo)
        yield
        C = GLA_CHUNK
        for c in range(T // C):
            rows = slice(c * C, (c + 1) * C)
            bc = b[rows]
            bl = bc[C - 1:C, :]
            fill.qe[rows, :] = (gq[rows] * jnp.exp(bc)).astype(BF16)
            fill.ke[rows, :] = (gk[rows] * jnp.exp(-bc)).astype(BF16)
            fill.kw[rows, :] = (gk[rows] * jnp.exp(bl - bc)).astype(BF16)
            fill.ea[c:c + 1, :] = jnp.exp(bl)

    def act_silu(p, sl):
        return _silu(p)

    def act_silu_gain(p, sl):
        return _silu(p) * gg_ref[:, sl]

    stage_norm = [a_norm(0), a_norm(1)]
    stage_proj = [
        a_rotary(O_RQ, fill.q, fill.qx, xi_ref, 0), a_rotary(O_RQ, fill.q, fill.qx, xi_ref, 1),
        a_rotary(O_RK, fill.k, fill.kz, zeta_ref, 0), a_rotary(O_RK, fill.k, fill.kz, zeta_ref, 1),
        a_plain(O_RV, fill.v, 0), a_plain(O_RV, fill.v, 1),
        a_plain(O_RZ, fill.rg, 0, act_silu), a_plain(O_RZ, fill.rg, 1, act_silu),
        a_gla_prep,
        a_plain(O_GV, fill.gv, 0), a_plain(O_GV, fill.gv, 1),
        a_plain(O_GZ, fill.gg, 0, act_silu_gain), a_plain(O_GZ, fill.gg, 1, act_silu_gain),
    ]

    def b_ret_block(ci, hh):
        def piece():
            rows = pl.ds(ci * RET_CHUNK, RET_CHUNK)
            sl = slice(hh * RET_DK, (hh + 1) * RET_DK)
            q = drain.q[rows, sl]
            k = drain.k[rows, sl]
            v = drain.v[rows, sl]
            sc = _dot_nt(q, k)
            kv = _dot_tn(drain.kz[rows, sl], v)
            yield
            state = r_state[hh]
            if ci == 0:
                state = jnp.where(new_sequence, 0.0, state)
            lhs = jnp.concatenate([(sc * decay_ref[hh]).astype(BF16), drain.qx[rows, sl]], axis=1)
            rhs = jnp.concatenate([v, state.astype(BF16)], axis=0)
            y = _dot(lhs, rhs)
            r_state[hh] = state * ret_chunk_decay[hh] + kv
            yield
            fill.mix[rows, sl] = (_rms_lanes(y) * drain.rg[rows, sl]).astype(BF16)
        return piece

    C = GLA_CHUNK
    HC = GLA_HEADS * C

    def b_gla_chunk(c):
        def piece():
            rows = pl.ds(c * C, C)
            head_lanes = hmask_ref[...] > 0
            zero = jnp.zeros((HC, GLA_DK_TOTAL), BF16)

            def stack(ref):
                t = ref[rows, :]
                return jnp.where(head_lanes, jnp.concatenate([t] * GLA_HEADS, axis=0), zero)

            q_st = stack(drain.qe)
            k_st = stack(drain.ke)
            w_st = stack(drain.kw)
            v = drain.gv[rows, :]
            v_st = jnp.concatenate([v[:, hh * GLA_DV:(hh + 1) * GLA_DV] for hh in range(GLA_HEADS)], axis=0)
            sc = _dot_nt(drain.qe[rows, :], k_st)
            upd = _dot_tn(v_st, w_st)
            yield
            r_i = lax.broadcasted_iota(jnp.int32, (C, HC), 0)
            c_i = lax.broadcasted_iota(jnp.int32, (C, HC), 1) & (C - 1)
            sc = jnp.where(r_i >= c_i, sc, 0.0).astype(BF16)
            v_bd = jnp.where(vmask_ref[...] > 0, jnp.concatenate([v] * GLA_HEADS, axis=0),
                             jnp.zeros((HC, D_GLA), BF16))
            inner = _dot(sc, v_bd)
            state_t = g_state[...]
            if c == 0:
                state_t = jnp.where(new_sequence, 0.0, state_t)
            cross = _dot_nt(q_st, state_t.astype(BF16))
            g_state[...] = state_t * drain.ea[c:c + 1, :] + upd
            yield
            for hh in range(GLA_HEADS):
                sl = slice(hh * GLA_DV, (hh + 1) * GLA_DV)
                y = inner[:, sl] + cross[hh * C:(hh + 1) * C]
                fill.mix[rows, D_RET + hh * GLA_DV:D_RET + (hh + 1) * GLA_DV] = (
                    _rms_lanes(y) * drain.gg[rows, sl]).astype(BF16)
        return piece

    def b_out_proj(n):
        def piece():
            sl = slice(n * MXU_COLS, (n + 1) * MXU_COLS)
            o = _dot(drain.mix[...], wout_ref[:, sl])
            yield
            xn = xb_ref[0, out_rows, sl] + ada_ref[batch_out, 2:3, sl] * o
            o_ref[0, out_rows, sl] = xn
            part = jnp.sum(xn * xn, axis=-1, keepdims=True)
            if n == 0:
                ssq_s[out_rows, :] = part
            else:
                ssq_s[out_rows, :] += part
        return piece

    def b_final_norm(half):
        def piece():
            rows = slice(out_rows.start + half * (T // 2), out_rows.start + (half + 1) * (T // 2))
            scale = lax.rsqrt(ssq_s[rows, :] * (1.0 / D_MODEL) + EPS)
            o_ref[0, rows, :] = o_ref[0, rows, :] * scale * fg_ref[...]
            yield
        return piece

    ret_blocks = [b_ret_block(ci, hh) for ci in range(T // RET_CHUNK) for hh in range(RET_HEADS)]
    gla_chunks = [b_gla_chunk(c) for c in range(T // C)]
    stage_mix = ret_blocks + gla_chunks
    out_projs = [b_out_proj(n) for n in range(D_MODEL // MXU_COLS)]
    final_norms = [b_final_norm(0), b_final_norm(1)]
    for piece in final_norms:
        piece.after = out_projs
    stage_out = out_projs + final_norms

    return _merge_stages(dict(p=stage_proj, m=stage_mix, o=stage_out, n=stage_norm), EMIT_ORDER)


def _retention_tables(T):
    H, C = RET_HEADS, RET_CHUNK
    log_g = np.log(1.0 - 2.0 ** (-5.0 - np.arange(H, dtype=np.float64)))
    idx = np.arange(C, dtype=np.float64)
    diff = idx[:, None] - idx[None, :]
    k_scale = RET_DK ** -0.5
    decay = np.where(diff[None] >= 0, np.exp(np.maximum(diff, 0.0)[None] * log_g[:, None, None]), 0.0)
    zeta = np.exp((C - 1.0 - idx)[None, :] * log_g[:, None])
    xi = np.exp((idx + 1.0)[None, :] * log_g[:, None])
    chunk_decay = tuple(float(v) for v in np.exp(C * log_g))

    def full(t):
        return np.tile(np.repeat(t.T, RET_DV, axis=1), (T // C, 1))

    return ((decay * k_scale).astype(np.float32), full(xi).astype(np.float32),
            (full(zeta) * k_scale).astype(np.float32), chunk_decay)


def _gla_tables(T):
    C = GLA_CHUNK
    r = np.arange(T)
    tri = ((r[:, None] // C == r[None, :] // C) & (r[:, None] >= r[None, :])).astype(np.float32)
    row_head = np.arange(GLA_HEADS * C) // C
    lane_head = np.arange(GLA_DK_TOTAL) // GLA_DK
    hmask = (row_head[:, None] == lane_head[None, :]).astype(np.float32)
    value_head = np.arange(D_GLA) // GLA_DV
    vmask = (row_head[:, None] == value_head[None, :]).astype(np.float32)
    return tri, hmask, vmask


def kernel(x, c, norm_gain, w_ada, b_ada, w_in, w_gate_up, b_gate_up, gla_norm_gain, w_out, final_gain):
    B, S, D = x.shape
    T = SEQ_TILE
    K = STEPS_PER_GRID_STEP
    assert D == D_MODEL and S % (K * T) == 0 and T % RET_CHUNK == 0 and (OUT_LAG + 1) % K == 0 and K % 2 == 0
    tiles_per_seq = S // T
    n_tiles = B * tiles_per_seq

    ada = pl.pallas_call(
        _ada_kernel,
        grid=(3 * D // ADA_COLS,),
        in_specs=[pl.BlockSpec((B, D), lambda j: (0, 0)),
                  pl.BlockSpec((None, D, ADA_COLS), lambda j: (0, 0, j)),
                  pl.BlockSpec((1, ADA_COLS), lambda j: (0, j))],
        out_specs=pl.BlockSpec((B, ADA_COLS), lambda j: (0, j)),
        out_shape=jax.ShapeDtypeStruct((B, 3 * D), F32),
        name="adaln_proj",
    )(c, w_ada, b_ada)
    ada3 = ada.reshape(B, 3, D)

    w_in_b = w_in.astype(BF16)
    w_out_b = w_out.astype(BF16)
    wgu_b = w_gate_up.astype(BF16)
    gg = jnp.tile(gla_norm_gain[0], GLA_HEADS)[None, :]

    inv_freq = ROPE_THETA ** (-jnp.arange(0, RET_DK, 2, dtype=F32) / RET_DK)
    ang = jnp.arange(S, dtype=F32)[:, None] * inv_freq[None, :]
    cos2 = jnp.concatenate([jnp.cos(ang), jnp.cos(ang)], axis=-1)
    sin2 = jnp.concatenate([-jnp.sin(ang), jnp.sin(ang)], axis=-1)

    decay, xi_full, zeta_full, chunk_decay = _retention_tables(T)
    tri, hmask, vmask = _gla_tables(T)

    def tile(step, lag):
        t = jnp.clip(step - lag, 0, n_tiles - 1)
        return t // tiles_per_seq, t % tiles_per_seq

    def out_block(j):
        p = jnp.clip(j - (OUT_LAG + 1) // K, 0, n_tiles // K - 1)
        return p // (tiles_per_seq // K), p % (tiles_per_seq // K), 0

    def const(rank):
        return dict(index_map=lambda j: (0,) * rank, pipeline_mode=pl.Buffered(1))

    const2, const3 = const(2), const(3)
    body = functools.partial(_layer_kernel, ret_chunk_decay=chunk_decay,
                             n_tiles=n_tiles, tiles_per_seq=tiles_per_seq)
    return pl.pallas_call(
        body,
        grid=((n_tiles + OUT_LAG + 1) // K,),
        in_specs=[
            pl.BlockSpec((1, T, D), functools.partial(lambda k, j: (*tile(K * j - 1 + k, 0), 0), k))
            for k in range(K)
        ] + [
            pl.BlockSpec((1, K * T, D), out_block),
            pl.BlockSpec((B, 3, D), **const3),
            pl.BlockSpec((1, D), **const2),
            pl.BlockSpec((1, D), **const2),
            pl.BlockSpec((1, D_GLA), **const2),
            pl.BlockSpec((1, GLA_DK_TOTAL), **const2),
            pl.BlockSpec((None, GLA_GATE_RANK, GLA_DK_TOTAL), **const3),
            pl.BlockSpec((None, D, D_IN), **const3),
            pl.BlockSpec((None, D, D), **const3),
            pl.BlockSpec((S, RET_DK), **const2),
            pl.BlockSpec((S, RET_DK), **const2),
            pl.BlockSpec((RET_HEADS, RET_CHUNK, RET_CHUNK), **const3),
            pl.BlockSpec((T, D_RET), **const2),
            pl.BlockSpec((T, D_RET), **const2),
            pl.BlockSpec((T, T), **const2),
            pl.BlockSpec((GLA_HEADS * GLA_CHUNK, GLA_DK_TOTAL), **const2),
            pl.BlockSpec((GLA_HEADS * GLA_CHUNK, D_GLA), **const2),
        ],
        out_specs=pl.BlockSpec((1, K * T, D), out_block),
        out_shape=jax.ShapeDtypeStruct((B, S, D), F32),
        scratch_shapes=[
            pltpu.VMEM((K * T, 1), F32),
            pltpu.VMEM((RET_HEADS, RET_DK, RET_DV), F32),
            pltpu.VMEM((GLA_DV, GLA_HEADS * GLA_DK), F32),
        ] + _slot_scratch(T) + _slot_scratch(T),
        compiler_params=pltpu.CompilerParams(
            dimension_semantics=("arbitrary",),
            vmem_limit_bytes=VMEM_LIMIT_BYTES),
        name="retention_gla_layer",
    )(*([x] * K), x, ada3, norm_gain, final_gain[None, :], gg, b_gate_up, wgu_b, w_in_b, w_out_b,
      cos2, sin2, jnp.asarray(decay), jnp.asarray(xi_full), jnp.asarray(zeta_full),
      jnp.asarray(tri, dtype=BF16), jnp.asarray(hmask, dtype=BF16), jnp.asarray(vmask, dtype=BF16))
```

```python
import functools
from typing import NamedTuple

import numpy as np
import jax
import jax.numpy as jnp
from jax import lax
from jax.experimental import pallas as pl
from jax.experimental.pallas import tpu as pltpu

D_MODEL = 1024
D_RET = 512
D_GLA = 512
RET_HEADS = 4
RET_DK = 128
RET_DV = 128
RET_CHUNK = 128
GLA_HEADS = 4
GLA_DK_TOTAL = 256
GLA_DK = 64
GLA_DV = 128
GLA_CHUNK = 64
GLA_GATE_RANK = 16
GLA_GATE_TAU = 16.0
ROPE_THETA = 10000.0
EPS = 1e-6
D_IN = 4 * D_RET + 2 * GLA_DK_TOTAL + 2 * D_GLA + GLA_GATE_RANK

MXU_COLS = 256
ADA_COLS = 2 * MXU_COLS

O_RQ, O_RK, O_RV, O_RZ = 0, 512, 1024, 1536
O_GQ, O_GK, O_GV, O_GZ, O_GLR = 2048, 2304, 2560, 3072, 3584

SEQ_TILE = 256
PROJ_LAG, MIX_LAG, OUT_LAG = 1, 2, 3
EMIT_ORDER = "pmopmpmonpmpmopmpmpompmponmpmpomp"
PIECES_IN_FLIGHT = 16
VMEM_LIMIT_BYTES = 48 * 1024 * 1024

F32 = jnp.float32
BF16 = jnp.bfloat16


class _Slot(NamedTuple):
    hb: object
    q: object
    qx: object
    k: object
    kz: object
    v: object
    rg: object
    qe: object
    ke: object
    kw: object
    gv: object
    gg: object
    ea: object
    mix: object


def _slot_scratch(T):
    shapes = dict(hb=((T, D_MODEL), BF16), mix=((T, D_MODEL), BF16), q=((T, D_RET), BF16),
                  qx=((T, D_RET), BF16), k=((T, D_RET), BF16), kz=((T, D_RET), BF16),
                  v=((T, D_RET), BF16), rg=((T, D_RET), BF16),
                  qe=((T, GLA_DK_TOTAL), BF16), ke=((T, GLA_DK_TOTAL), BF16), kw=((T, GLA_DK_TOTAL), BF16),
                  gv=((T, D_GLA), BF16), gg=((T, D_GLA), BF16), ea=((8, GLA_DK_TOTAL), F32))
    return [pltpu.VMEM(*shapes[f]) for f in _Slot._fields]


def _split2(a):
    hi = a.astype(BF16)
    lo = (a - hi.astype(F32)).astype(BF16)
    return hi, lo


def _dot(a, b):
    return jnp.dot(a, b, preferred_element_type=F32)


def _dot_nt(a, b):
    return lax.dot_general(a, b, (((1,), (1,)), ((), ())), preferred_element_type=F32)


def _dot_tn(a, b):
    return lax.dot_general(a, b, (((0,), (0,)), ((), ())), preferred_element_type=F32)


def _silu(z):
    return z * jax.nn.sigmoid(z)


def _ada_kernel(c_ref, w_ref, b_ref, o_ref):
    cs = _silu(c_ref[...])
    w = w_ref[...]
    c_hi = cs.astype(BF16)
    c_lo = (cs - c_hi.astype(F32)).astype(BF16)
    w_hi = w.astype(BF16)
    w_lo = (w - w_hi.astype(F32)).astype(BF16)
    o_ref[...] = _dot(c_hi, w_hi) + _dot(c_lo, w_hi) + _dot(c_hi, w_lo) + b_ref[...]


def _rms_lanes(y):
    return y * lax.rsqrt(jnp.mean(y * y, axis=-1, keepdims=True) + EPS)


def _merge_stages(stages, order):
    assert sorted(order) == sorted(k for k, pieces in stages.items() for _ in pieces)
    queues = {k: iter(pieces) for k, pieces in stages.items()}
    return [next(queues[k]) for k in order]


def _emit_in_flight(pieces, width):
    pending = list(pieces)
    active, done = [], set()
    while pending or active:
        for piece in list(pending):
            if len(active) == width:
                break
            if all(dep in done for dep in getattr(piece, "after", ())):
                pending.remove(piece)
                active.append((piece, piece()))
        assert active, "a piece waits on one that is never emitted"
        for piece, gen in list(active):
            if next(gen, StopIteration) is StopIteration:
                active.remove((piece, gen))
                done.add(piece)


def _layer_kernel(xa_odd_ref, xa_even_ref, xb_ref, ada_ref, ng_ref, fg_ref, gg_ref, bgu_ref, wgu_ref, win_ref,
                  wout_ref, cos_ref, sin_ref, decay_ref, xi_ref, zeta_ref, tri_ref, hmask_ref, vmask_ref,
                  o_ref,
                  ssq_s, r_state, g_state, *slot_refs,
                  ret_chunk_decay, n_tiles, tiles_per_seq):
    j = pl.program_id(0)
    slots = (_Slot(*slot_refs[:len(_Slot._fields)]), _Slot(*slot_refs[len(_Slot._fields):]))

    @pl.when(j == 0)
    def _():
        for ref in slot_refs + (r_state, g_state):
            ref[...] = jnp.zeros_like(ref)

    T = xa_odd_ref.shape[1]
    for half, xa_ref in enumerate((xa_odd_ref, xa_even_ref)):
        parity = 1 - half
        pieces = _pipeline_step(
            2 * j - 1 + half, slice(half * T, (half + 1) * T),
            xa_ref, xb_ref, ada_ref, ng_ref, fg_ref, gg_ref, bgu_ref, wgu_ref, win_ref,
            wout_ref, cos_ref, sin_ref, decay_ref, xi_ref, zeta_ref, tri_ref, hmask_ref, vmask_ref,
            o_ref, ssq_s, r_state, g_state,
            fill=slots[parity], drain=slots[1 - parity],
            ret_chunk_decay=ret_chunk_decay, n_tiles=n_tiles, tiles_per_seq=tiles_per_seq)
        _emit_in_flight(pieces, PIECES_IN_FLIGHT)


def _pipeline_step(step, out_rows,
                   xa_ref, xb_ref, ada_ref, ng_ref, fg_ref, gg_ref, bgu_ref, wgu_ref, win_ref, wout_ref,
                   cos_ref, sin_ref, decay_ref, xi_ref, zeta_ref, tri_ref, hmask_ref, vmask_ref,
                   o_ref, ssq_s, r_state, g_state,
                   *, fill, drain, ret_chunk_decay, n_tiles, tiles_per_seq):
    T = xa_ref.shape[1]

    def tile(lag):
        return jnp.clip(step - lag, 0, n_tiles - 1)

    batch_norm = tile(0) // tiles_per_seq
    rope_rows = pl.ds(pl.multiple_of((tile(PROJ_LAG) % tiles_per_seq) * T, T), T)
    new_sequence = tile(MIX_LAG) % tiles_per_seq == 0
    batch_out = tile(OUT_LAG) // tiles_per_seq

    def a_norm(half):
        def piece():
            rows = slice(half * (T // 2), (half + 1) * (T // 2))
            x = xa_ref[0, rows, :]
            shift = ada_ref[batch_norm, 0:1, :]
            scale = ada_ref[batch_norm, 1:2, :]
            h = _rms_lanes(x) * (ng_ref[...] * (1.0 + scale)) + shift
            fill.hb[rows, :] = h.astype(BF16)
            yield
        return piece

    def proj(lo, width):
        return _dot(drain.hb[...], win_ref[:, lo:lo + width])

    def a_rotary(col, out_s, outw_s, w_ref, pair):
        def piece():
            cos = cos_ref[rope_rows, :]
            sin = sin_ref[rope_rows, :]
            p = proj(col + pair * MXU_COLS, MXU_COLS)
            yield
            for j in range(2):
                hh = 2 * pair + j
                sl = slice(hh * RET_DK, (hh + 1) * RET_DK)
                t = p[:, j * RET_DK:(j + 1) * RET_DK]
                r = t * cos + pltpu.roll(t, RET_DK // 2, axis=1) * sin
                out_s[:, sl] = r.astype(BF16)
                outw_s[:, sl] = (r * w_ref[:, sl]).astype(BF16)
        return piece

    def a_plain(col, out_s, half, act=None):
        def piece():
            sl = slice(half * MXU_COLS, (half + 1) * MXU_COLS)
            p = proj(col + half * MXU_COLS, MXU_COLS)
            yield
            if act is not None:
                p = act(p, sl)
            out_s[:, sl] = p.astype(out_s.dtype)
        return piece

    def a_gla_prep():
        gq = proj(O_GQ, GLA_DK_TOTAL) * (GLA_DK ** -0.5)
        gk = proj(O_GK, GLA_DK_TOTAL)
        glr = proj(O_GLR, GLA_GATE_RANK)
        yield
        zg = _dot(glr.astype(BF16), wgu_ref[...]) + bgu_ref[...]
        yield
        la = (jnp.minimum(zg, 0.0) - jnp.log(1.0 + jnp.exp(-jnp.abs(zg)))) * (1.0 / GLA_GATE_TAU)
        hi, lo = _split2(la)
        yield
        tri = tri_ref[...]
        b = _dot(tri, hi) + _dot(tri, lo)
        yield
        C = GLA_CHUNK
        for c in range(T // C):
            rows = slice(c * C, (c + 1) * C)
            bc = b[rows]
            bl = bc[C - 1:C, :]
            fill.qe[rows, :] = (gq[rows] * jnp.exp(bc)).astype(BF16)
            fill.ke[rows, :] = (gk[rows] * jnp.exp(-bc)).astype(BF16)
            fill.kw[rows, :] = (gk[rows] * jnp.exp(bl - bc)).astype(BF16)
            fill.ea[c:c + 1, :] = jnp.exp(bl)

    def act_silu(p, sl):
        return _silu(p)

    def act_silu_gain(p, sl):
        return _silu(p) * gg_ref[:, sl]

    stage_norm = [a_norm(0), a_norm(1)]
    stage_proj = [
        a_rotary(O_RQ, fill.q, fill.qx, xi_ref, 0), a_rotary(O_RQ, fill.q, fill.qx, xi_ref, 1),
        a_rotary(O_RK, fill.k, fill.kz, zeta_ref, 0), a_rotary(O_RK, fill.k, fill.kz, zeta_ref, 1),
        a_plain(O_RV, fill.v, 0), a_plain(O_RV, fill.v, 1),
        a_plain(O_RZ, fill.rg, 0, act_silu), a_plain(O_RZ, fill.rg, 1, act_silu),
        a_gla_prep,
        a_plain(O_GV, fill.gv, 0), a_plain(O_GV, fill.gv, 1),
        a_plain(O_GZ, fill.gg, 0, act_silu_gain), a_plain(O_GZ, fill.gg, 1, act_silu_gain),
    ]

    def b_ret_block(ci, hh):
        def piece():
            rows = pl.ds(ci * RET_CHUNK, RET_CHUNK)
            sl = slice(hh * RET_DK, (hh + 1) * RET_DK)
            q = drain.q[rows, sl]
            k = drain.k[rows, sl]
            v = drain.v[rows, sl]
            sc = _dot_nt(q, k)
            kv = _dot_tn(drain.kz[rows, sl], v)
            yield
            state = r_state[hh]
            if ci == 0:
                state = jnp.where(new_sequence, 0.0, state)
            lhs = jnp.concatenate([(sc * decay_ref[hh]).astype(BF16), drain.qx[rows, sl]], axis=1)
            rhs = jnp.concatenate([v, state.astype(BF16)], axis=0)
            y = _dot(lhs, rhs)
            r_state[hh] = state * ret_chunk_decay[hh] + kv
            yield
            fill.mix[rows, sl] = (_rms_lanes(y) * drain.rg[rows, sl]).astype(BF16)
        return piece

    C = GLA_CHUNK
    HC = GLA_HEADS * C

    def b_gla_chunk(c):
        def piece():
            rows = pl.ds(c * C, C)
            head_lanes = hmask_ref[...] > 0
            zero = jnp.zeros((HC, GLA_DK_TOTAL), BF16)

            def stack(ref):
                t = ref[rows, :]
                return jnp.where(head_lanes, jnp.concatenate([t] * GLA_HEADS, axis=0), zero)

            q_st = stack(drain.qe)
            k_st = stack(drain.ke)
            w_st = stack(drain.kw)
            v = drain.gv[rows, :]
            v_st = jnp.concatenate([v[:, hh * GLA_DV:(hh + 1) * GLA_DV] for hh in range(GLA_HEADS)], axis=0)
            sc = _dot_nt(drain.qe[rows, :], k_st)
            upd = _dot_tn(v_st, w_st)
            yield
            r_i = lax.broadcasted_iota(jnp.int32, (C, HC), 0)
            c_i = lax.broadcasted_iota(jnp.int32, (C, HC), 1) & (C - 1)
            sc = jnp.where(r_i >= c_i, sc, 0.0).astype(BF16)
            v_bd = jnp.where(vmask_ref[...] > 0, jnp.concatenate([v] * GLA_HEADS, axis=0),
                             jnp.zeros((HC, D_GLA), BF16))
            inner = _dot(sc, v_bd)
            state_t = g_state[...]
            if c == 0:
                state_t = jnp.where(new_sequence, 0.0, state_t)
            cross = _dot_nt(q_st, state_t.astype(BF16))
            g_state[...] = state_t * drain.ea[c:c + 1, :] + upd
            yield
            for hh in range(GLA_HEADS):
                sl = slice(hh * GLA_DV, (hh + 1) * GLA_DV)
                y = inner[:, sl] + cross[hh * C:(hh + 1) * C]
                fill.mix[rows, D_RET + hh * GLA_DV:D_RET + (hh + 1) * GLA_DV] = (
                    _rms_lanes(y) * drain.gg[rows, sl]).astype(BF16)
        return piece

    def b_out_proj(n):
        def piece():
            sl = slice(n * MXU_COLS, (n + 1) * MXU_COLS)
            o = _dot(drain.mix[...], wout_ref[:, sl])
            yield
            xn = xb_ref[0, out_rows, sl] + ada_ref[batch_out, 2:3, sl] * o
            o_ref[0, out_rows, sl] = xn
            part = jnp.sum(xn * xn, axis=-1, keepdims=True)
            if n == 0:
                ssq_s[out_rows, :] = part
            else:
                ssq_s[out_rows, :] += part
        return piece

    def b_final_norm(half):
        def piece():
            rows = slice(out_rows.start + half * (T // 2), out_rows.start + (half + 1) * (T // 2))
            scale = lax.rsqrt(ssq_s[rows, :] * (1.0 / D_MODEL) + EPS)
            o_ref[0, rows, :] = o_ref[0, rows, :] * scale * fg_ref[...]
            yield
        return piece

    ret_blocks = [b_ret_block(ci, hh) for ci in range(T // RET_CHUNK) for hh in range(RET_HEADS)]
    gla_chunks = [b_gla_chunk(c) for c in range(T // C)]
    stage_mix = ret_blocks + gla_chunks
    out_projs = [b_out_proj(n) for n in range(D_MODEL // MXU_COLS)]
    final_norms = [b_final_norm(0), b_final_norm(1)]
    for piece in final_norms:
        piece.after = out_projs
    stage_out = out_projs + final_norms

    return _merge_stages(dict(p=stage_proj, m=stage_mix, o=stage_out, n=stage_norm), EMIT_ORDER)


def _retention_tables(T):
    H, C = RET_HEADS, RET_CHUNK
    log_g = np.log(1.0 - 2.0 ** (-5.0 - np.arange(H, dtype=np.float64)))
    idx = np.arange(C, dtype=np.float64)
    diff = idx[:, None] - idx[None, :]
    k_scale = RET_DK ** -0.5
    decay = np.where(diff[None] >= 0, np.exp(np.maximum(diff, 0.0)[None] * log_g[:, None, None]), 0.0)
    zeta = np.exp((C - 1.0 - idx)[None, :] * log_g[:, None])
    xi = np.exp((idx + 1.0)[None, :] * log_g[:, None])
    chunk_decay = tuple(float(v) for v in np.exp(C * log_g))

    def full(t):
        return np.tile(np.repeat(t.T, RET_DV, axis=1), (T // C, 1))

    return ((decay * k_scale).astype(np.float32), full(xi).astype(np.float32),
            (full(zeta) * k_scale).astype(np.float32), chunk_decay)


def _gla_tables(T):
    C = GLA_CHUNK
    r = np.arange(T)
    tri = ((r[:, None] // C == r[None, :] // C) & (r[:, None] >= r[None, :])).astype(np.float32)
    row_head = np.arange(GLA_HEADS * C) // C
    lane_head = np.arange(GLA_DK_TOTAL) // GLA_DK
    hmask = (row_head[:, None] == lane_head[None, :]).astype(np.float32)
    value_head = np.arange(D_GLA) // GLA_DV
    vmask = (row_head[:, None] == value_head[None, :]).astype(np.float32)
    return tri, hmask, vmask


def kernel(x, c, norm_gain, w_ada, b_ada, w_in, w_gate_up, b_gate_up, gla_norm_gain, w_out, final_gain):
    B, S, D = x.shape
    T = SEQ_TILE
    assert D == D_MODEL and S % (2 * T) == 0 and T % RET_CHUNK == 0 and OUT_LAG % 2 == 1
    tiles_per_seq = S // T
    n_tiles = B * tiles_per_seq

    ada = pl.pallas_call(
        _ada_kernel,
        grid=(3 * D // ADA_COLS,),
        in_specs=[pl.BlockSpec((B, D), lambda j: (0, 0)),
                  pl.BlockSpec((D, ADA_COLS), lambda j: (0, j)),
                  pl.BlockSpec((1, ADA_COLS), lambda j: (0, j))],
        out_specs=pl.BlockSpec((B, ADA_COLS), lambda j: (0, j)),
        out_shape=jax.ShapeDtypeStruct((B, 3 * D), F32),
        name="adaln_proj",
    )(c, w_ada[0], b_ada[0][None, :])
    ada3 = ada.reshape(B, 3, D)

    w_in_b = w_in[0].astype(BF16)
    w_out_b = w_out[0].astype(BF16)
    wgu_b = w_gate_up[0].astype(BF16)
    bgu = b_gate_up[0][None, :]
    gg = jnp.tile(gla_norm_gain[0], GLA_HEADS)[None, :]

    inv_freq = ROPE_THETA ** (-jnp.arange(0, RET_DK, 2, dtype=F32) / RET_DK)
    ang = jnp.arange(S, dtype=F32)[:, None] * inv_freq[None, :]
    cos2 = jnp.concatenate([jnp.cos(ang), jnp.cos(ang)], axis=-1)
    sin2 = jnp.concatenate([-jnp.sin(ang), jnp.sin(ang)], axis=-1)

    decay, xi_full, zeta_full, chunk_decay = _retention_tables(T)
    tri, hmask, vmask = _gla_tables(T)

    def tile(step, lag):
        t = jnp.clip(step - lag, 0, n_tiles - 1)
        return t // tiles_per_seq, t % tiles_per_seq

    def out_block(j):
        p = jnp.clip(j - (OUT_LAG + 1) // 2, 0, n_tiles // 2 - 1)
        return p // (tiles_per_seq // 2), p % (tiles_per_seq // 2), 0

    def const(rank):
        return dict(index_map=lambda j: (0,) * rank, pipeline_mode=pl.Buffered(1))

    const2, const3 = const(2), const(3)
    body = functools.partial(_layer_kernel, ret_chunk_decay=chunk_decay,
                             n_tiles=n_tiles, tiles_per_seq=tiles_per_seq)
    return pl.pallas_call(
        body,
        grid=(n_tiles // 2 + (OUT_LAG + 1) // 2,),
        in_specs=[
            pl.BlockSpec((1, T, D), lambda j: (*tile(2 * j - 1, 0), 0)),
            pl.BlockSpec((1, T, D), lambda j: (*tile(2 * j, 0), 0)),
            pl.BlockSpec((1, 2 * T, D), out_block),
            pl.BlockSpec((B, 3, D), **const3),
            pl.BlockSpec((1, D), **const2),
            pl.BlockSpec((1, D), **const2),
            pl.BlockSpec((1, D_GLA), **const2),
            pl.BlockSpec((1, GLA_DK_TOTAL), **const2),
            pl.BlockSpec((GLA_GATE_RANK, GLA_DK_TOTAL), **const2),
            pl.BlockSpec((D, D_IN), **const2),
            pl.BlockSpec((D, D), **const2),
            pl.BlockSpec((S, RET_DK), **const2),
            pl.BlockSpec((S, RET_DK), **const2),
            pl.BlockSpec((RET_HEADS, RET_CHUNK, RET_CHUNK), **const3),
            pl.BlockSpec((T, D_RET), **const2),
            pl.BlockSpec((T, D_RET), **const2),
            pl.BlockSpec((T, T), **const2),
            pl.BlockSpec((GLA_HEADS * GLA_CHUNK, GLA_DK_TOTAL), **const2),
            pl.BlockSpec((GLA_HEADS * GLA_CHUNK, D_GLA), **const2),
        ],
        out_specs=pl.BlockSpec((1, 2 * T, D), out_block),
        out_shape=jax.ShapeDtypeStruct((B, S, D), F32),
        scratch_shapes=[
            pltpu.VMEM((2 * T, 1), F32),
            pltpu.VMEM((RET_HEADS, RET_DK, RET_DV), F32),
            pltpu.VMEM((GLA_DV, GLA_HEADS * GLA_DK), F32),
        ] + _slot_scratch(T) + _slot_scratch(T),
        compiler_params=pltpu.CompilerParams(
            dimension_semantics=("arbitrary",),
            vmem_limit_bytes=VMEM_LIMIT_BYTES),
        name="retention_gla_layer",
    )(x, x, x, ada3, norm_gain, final_gain[None, :], gg, bgu, wgu_b, w_in_b, w_out_b,
      cos2, sin2, jnp.asarray(decay), jnp.asarray(xi_full), jnp.asarray(zeta_full),
      jnp.asarray(tri, dtype=BF16), jnp.asarray(hmask, dtype=BF16), jnp.asarray(vmask, dtype=BF16))
```
